```python
import math
import jax
import jax.numpy as jnp
from jax import lax
import numpy as np


D_MODEL = 1024
BATCH = 2
SEQ = 8192
DEPTH = 4

PLE_DIM = 256
N_EVEN = (DEPTH + 1) // 2
N_ODD = DEPTH // 2
D_FF = 2816
NORM_EPS = 1e-6
LN_EPS = 1e-5

GM_CHUNK = 128
GM_HEADS = 8
GM_HEAD_DIM = 128
GM_WIDTH = GM_HEADS * GM_HEAD_DIM

SSD_HEAD_DIM = 64
SSD_HEADS = 16
SSD_INNER = SSD_HEADS * SSD_HEAD_DIM
SSD_GROUPS = 2
SSD_STATE = 128
SSD_CONV = 4
SSD_CHUNK = 128
SSD_CONV_CH = SSD_INNER + 2 * SSD_GROUPS * SSD_STATE

HYB_IN = 2 * GM_WIDTH + SSD_INNER + SSD_CONV_CH + SSD_HEADS
HYB_OUT = GM_WIDTH + SSD_INNER

MLA_HEADS = 16
MLA_NOPE = 128
MLA_ROPE = 64
MLA_V = 128
MLA_Q_LORA = 256
MLA_KV_LORA = 128
MLA_QK = MLA_NOPE + MLA_ROPE
MLA_IN = MLA_Q_LORA + MLA_KV_LORA + MLA_ROPE
ROPE_BASE = 10000.0
ATTN_BLOCK = 128

kernel_name = "hybrid_gmlp_ssd_mla_macaron"


def rmsnorm(x, g):
    xf = x.astype(jnp.float32)
    y = xf * lax.rsqrt(jnp.mean(xf * xf, axis=-1, keepdims=True) + NORM_EPS)
    return (y * g.astype(jnp.float32)).astype(x.dtype)


def layernorm(x, g, b):
    xf = x.astype(jnp.float32)
    mu = jnp.mean(xf, axis=-1, keepdims=True)
    xc = xf - mu
    var = jnp.mean(xc * xc, axis=-1, keepdims=True)
    y = xc * lax.rsqrt(var + LN_EPS) * g.astype(jnp.float32) + b.astype(jnp.float32)
    return y.astype(x.dtype)


def swiglu(x, w_in, w_down):
    gate, up = jnp.split(x @ w_in, 2, axis=-1)
    return (jax.nn.silu(gate) * up) @ w_down


def rope_tables(positions):
    inv = 1.0 / (ROPE_BASE ** (jnp.arange(0, MLA_ROPE, 2, dtype=jnp.float32) / MLA_ROPE))
    ang = positions.astype(jnp.float32)[..., None] * inv
    return jnp.cos(ang), jnp.sin(ang)


def apply_rope(x, cos, sin):
    shape = cos.shape[:2] + (1,) * (x.ndim - 3) + cos.shape[2:]
    c = cos.reshape(shape)
    s = sin.reshape(shape)
    x1, x2 = jnp.split(x.astype(jnp.float32), 2, axis=-1)
    return jnp.concatenate([x1 * c - x2 * s, x2 * c + x1 * s], axis=-1).astype(x.dtype)


def gmlp_spatial_gating(uv, ln_g, ln_b, w_s, b_s):
    bsz, seq, _ = uv.shape
    u, v = jnp.split(jax.nn.gelu(uv), 2, axis=-1)
    v = v.reshape(bsz, seq // GM_CHUNK, GM_CHUNK, GM_HEADS, GM_HEAD_DIM)
    v = layernorm(v, ln_g.reshape(GM_HEADS, GM_HEAD_DIM), ln_b.reshape(GM_HEADS, GM_HEAD_DIM))
    causal = jnp.tril(jnp.ones((GM_CHUNK, GM_CHUNK), dtype=bool))
    w = jnp.where(causal[None], w_s, 0).astype(v.dtype)
    mixed = jnp.einsum('hts,bcshd->bcthd', w, v) + b_s.T[None, None, :, :, None]
    return u * mixed.reshape(bsz, seq, GM_WIDTH)


def ssd_chunked_scan(x, dt, a, bmat, cmat):
    bsz, seq, nh, hp = x.shape
    f32 = jnp.float32
    r = nh // SSD_GROUPS
    nc = seq // SSD_CHUNK
    L = SSD_CHUNK
    xd = (x.astype(f32) * dt[..., None]).reshape(bsz, nc, L, SSD_GROUPS, r, hp)
    da = (dt * a).reshape(bsz, nc, L, SSD_GROUPS, r)
    bc = bmat.astype(f32).reshape(bsz, nc, L, SSD_GROUPS, SSD_STATE)
    cc = cmat.astype(f32).reshape(bsz, nc, L, SSD_GROUPS, SSD_STATE)
    a_cs = jnp.cumsum(da, axis=2)
    seg = a_cs[:, :, :, None] - a_cs[:, :, None, :]
    causal = jnp.tril(jnp.ones((L, L), dtype=bool))[:, :, None, None]
    decay = jnp.exp(jnp.where(causal, seg, -jnp.inf))
    cb = jnp.einsum('bclgn,bcsgn->bclsg', cc, bc)
    y_diag = jnp.einsum('bclsgr,bcsgrp->bclgrp', cb[..., None] * decay, xd)
    decay_to_end = jnp.exp(a_cs[:, :, -1:] - a_cs)
    states = jnp.einsum('bclgn,bclgrp->bcgrpn', bc, xd * decay_to_end[..., None])
    chunk_decay = jnp.exp(a_cs[:, :, -1])

    def step(h, inp):
        st, dec = inp
        return h * dec[..., None, None] + st, h

    h0 = jnp.zeros((bsz, SSD_GROUPS, r, hp, SSD_STATE), f32)
    _, prev = lax.scan(step, h0, (jnp.moveaxis(states, 1, 0), jnp.moveaxis(chunk_decay, 1, 0)))
    prev = jnp.moveaxis(prev, 0, 1)
    y_off = jnp.einsum('bclgn,bcgrpn->bclgrp', cc, prev) * jnp.exp(a_cs)[..., None]
    return (y_diag + y_off).reshape(bsz, seq, nh, hp).astype(x.dtype)


def ssd_mixer(zxbcdt, conv_w, conv_b, dt_bias, a_log, d_skip, norm_g):
    bsz, seq, _ = zxbcdt.shape
    z, xbc, dt = jnp.split(zxbcdt, [SSD_INNER, SSD_INNER + SSD_CONV_CH], axis=-1)
    xbc = lax.conv_general_dilated(
        xbc, conv_w[:, None, :], window_strides=(1,), padding=[(SSD_CONV - 1, 0)],
        dimension_numbers=('NWC', 'WIO', 'NWC'), feature_group_count=SSD_CONV_CH) + conv_b
    xbc = jax.nn.silu(xbc)
    xs, bmat, cmat = jnp.split(xbc, [SSD_INNER, SSD_INNER + SSD_GROUPS * SSD_STATE], axis=-1)
    dt = jax.nn.softplus(dt.astype(jnp.float32) + dt_bias.astype(jnp.float32))
    a = -jnp.exp(a_log.astype(jnp.float32))
    xs = xs.reshape(bsz, seq, SSD_HEADS, SSD_HEAD_DIM)
    y = ssd_chunked_scan(xs, dt, a,
                         bmat.reshape(bsz, seq, SSD_GROUPS, SSD_STATE),
                         cmat.reshape(bsz, seq, SSD_GROUPS, SSD_STATE))
    y = (y + d_skip[:, None] * xs).reshape(bsz, seq, SSD_INNER)
    yg = (y * jax.nn.silu(z)).reshape(bsz, seq, SSD_GROUPS, SSD_INNER // SSD_GROUPS)
    yg = rmsnorm(yg, norm_g.reshape(SSD_GROUPS, SSD_INNER // SSD_GROUPS))
    return yg.reshape(bsz, seq, SSD_INNER)


def causal_attention_blocked(q, k, v):
    bsz, seq, nh, dk = q.shape
    nb = seq // ATTN_BLOCK
    scale = dk ** -0.5
    qb = jnp.moveaxis(q.reshape(bsz, nb, ATTN_BLOCK, nh, dk), 1, 0)
    k_pos = jnp.arange(seq)

    def one_block(args):
        qi, blk = args
        s = jnp.einsum('bthd,bshd->bhts', qi, k).astype(jnp.float32) * scale
        q_pos = blk * ATTN_BLOCK + jnp.arange(ATTN_BLOCK)
        s = jnp.where(k_pos[None, :] <= q_pos[:, None], s, -jnp.inf)
        pr = jax.nn.softmax(s, axis=-1).astype(v.dtype)
        return jnp.einsum('bhts,bshd->bthd', pr, v)

    out = lax.map(one_block, (qb, jnp.arange(nb)))
    return jnp.moveaxis(out, 0, 1).reshape(bsz, seq, nh, v.shape[-1])


def mla_attention(h, w_in, q_norm_g, kv_norm_g, w_uq, w_ukv, w_out, cos, sin):
    bsz, seq, _ = h.shape
    c_q, c_kv, k_rope = jnp.split(h @ w_in, [MLA_Q_LORA, MLA_Q_LORA + MLA_KV_LORA], axis=-1)
    q = (rmsnorm(c_q, q_norm_g) @ w_uq).reshape(bsz, seq, MLA_HEADS, MLA_QK)
    q_nope, q_rope = jnp.split(q, [MLA_NOPE], axis=-1)
    q = jnp.concatenate([q_nope, apply_rope(q_rope, cos, sin)], axis=-1)
    kv = (rmsnorm(c_kv, kv_norm_g) @ w_ukv).reshape(bsz, seq, MLA_HEADS, MLA_NOPE + MLA_V)
    k_nope, v = jnp.split(kv, [MLA_NOPE], axis=-1)
    k_rope = apply_rope(k_rope, cos, sin)
    k = jnp.concatenate(
        [k_nope, jnp.broadcast_to(k_rope[:, :, None, :], (bsz, seq, MLA_HEADS, MLA_ROPE))], axis=-1)
    o = causal_attention_blocked(q, k, v)
    return o.reshape(bsz, seq, MLA_HEADS * MLA_V) @ w_out


def setup_inputs(seed: int = 0) -> dict:
    key = jax.random.key(seed)
    ks = iter(jax.random.split(key, 48))
    f32 = jnp.float32

    def nrm(shape, scale):
        return jax.random.normal(next(ks), shape, f32) * scale

    def gain(shape):
        return 1.0 + nrm(shape, 0.02)

    x = nrm((BATCH, SEQ, D_MODEL), 1.0)
    p = nrm((DEPTH, BATCH, SEQ, PLE_DIM), 1.0)
    start = jax.random.randint(next(ks), (BATCH, 1), 0, 4096, dtype=jnp.int32)
    positions = start + jnp.arange(SEQ, dtype=jnp.int32)[None, :]

    ffn1_pre_g = gain((DEPTH, D_MODEL))
    ffn1_w_in = nrm((DEPTH, D_MODEL, 2 * D_FF), D_MODEL ** -0.5)
    ffn1_w_down = nrm((DEPTH, D_FF, D_MODEL), D_FF ** -0.5)
    ffn1_post_g = gain((DEPTH, D_MODEL))
    mix_pre_g = gain((DEPTH, D_MODEL))
    mix_post_g = gain((DEPTH, D_MODEL))
    ffn2_pre_g = gain((DEPTH, D_MODEL))
    ffn2_w_in = nrm((DEPTH, D_MODEL, 2 * D_FF), D_MODEL ** -0.5)
    ffn2_w_down = nrm((DEPTH, D_FF, D_MODEL), D_FF ** -0.5)
    ffn2_post_g = gain((DEPTH, D_MODEL))
    ple_pre_g = gain((DEPTH, D_MODEL))
    ple_w_gate = nrm((DEPTH, D_MODEL, D_MODEL), D_MODEL ** -0.5)
    ple_w_proj = nrm((DEPTH, PLE_DIM, D_MODEL), PLE_DIM ** -0.5)
    ple_post_g = gain((DEPTH, D_MODEL))

    hyb_w_in = nrm((N_EVEN, D_MODEL, HYB_IN), D_MODEL ** -0.5)
    gm_ln_g = gain((N_EVEN, GM_WIDTH))
    gm_ln_b = nrm((N_EVEN, GM_WIDTH), 0.02)
    gm_w_s = nrm((N_EVEN, GM_HEADS, GM_CHUNK, GM_CHUNK), GM_CHUNK ** -0.5)
    gm_b_s = gain((N_EVEN, GM_HEADS, GM_CHUNK))
    ssd_conv_w = nrm((N_EVEN, SSD_CONV, SSD_CONV_CH), SSD_CONV ** -0.5)
    ssd_conv_b = nrm((N_EVEN, SSD_CONV_CH), 0.02)
    dt0 = jnp.exp(jax.random.uniform(next(ks), (N_EVEN, SSD_HEADS), f32,
                                     math.log(1e-3), math.log(1e-1)))
    ssd_dt_bias = dt0 + jnp.log(-jnp.expm1(-dt0))
    ssd_a_log = jnp.log(jax.random.uniform(next(ks), (N_EVEN, SSD_HEADS), f32, 1.0, 16.0))
    ssd_d = gain((N_EVEN, SSD_HEADS))
    ssd_norm_g = gain((N_EVEN, SSD_INNER))
    hyb_w_out = nrm((N_EVEN, HYB_OUT, D_MODEL), HYB_OUT ** -0.5)

    mla_w_in = nrm((N_ODD, D_MODEL, MLA_IN), D_MODEL ** -0.5)
    mla_q_norm_g = gain((N_ODD, MLA_Q_LORA))
    mla_kv_norm_g = gain((N_ODD, MLA_KV_LORA))
    mla_w_uq = nrm((N_ODD, MLA_Q_LORA, MLA_HEADS * MLA_QK), MLA_Q_LORA ** -0.5)
    mla_w_ukv = nrm((N_ODD, MLA_KV_LORA, MLA_HEADS * (MLA_NOPE + MLA_V)), MLA_KV_LORA ** -0.5)
    mla_w_out = nrm((N_ODD, MLA_HEADS * MLA_V, D_MODEL), (MLA_HEADS * MLA_V) ** -0.5)

    return {
        "x": x, "p": p, "positions": positions,
        "ffn1_pre_g": ffn1_pre_g, "ffn1_w_in": ffn1_w_in, "ffn1_w_down": ffn1_w_down,
        "ffn1_post_g": ffn1_post_g, "mix_pre_g": mix_pre_g, "mix_post_g": mix_post_g,
        "ffn2_pre_g": ffn2_pre_g, "ffn2_w_in": ffn2_w_in, "ffn2_w_down": ffn2_w_down,
        "ffn2_post_g": ffn2_post_g, "ple_pre_g": ple_pre_g, "ple_w_gate": ple_w_gate,
        "ple_w_proj": ple_w_proj, "ple_post_g": ple_post_g,
        "hyb_w_in": hyb_w_in, "gm_ln_g": gm_ln_g, "gm_ln_b": gm_ln_b, "gm_w_s": gm_w_s,
        "gm_b_s": gm_b_s, "ssd_conv_w": ssd_conv_w, "ssd_conv_b": ssd_conv_b,
        "ssd_dt_bias": ssd_dt_bias, "ssd_a_log": ssd_a_log, "ssd_d": ssd_d,
        "ssd_norm_g": ssd_norm_g, "hyb_w_out": hyb_w_out,
        "mla_w_in": mla_w_in, "mla_q_norm_g": mla_q_norm_g, "mla_kv_norm_g": mla_kv_norm_g,
        "mla_w_uq": mla_w_uq, "mla_w_ukv": mla_w_ukv, "mla_w_out": mla_w_out,
    }


def reference(x, p, positions,
              ffn1_pre_g, ffn1_w_in, ffn1_w_down, ffn1_post_g, mix_pre_g, mix_post_g,
              ffn2_pre_g, ffn2_w_in, ffn2_w_down, ffn2_post_g,
              ple_pre_g, ple_w_gate, ple_w_proj, ple_post_g,
              hyb_w_in, gm_ln_g, gm_ln_b, gm_w_s, gm_b_s,
              ssd_conv_w, ssd_conv_b, ssd_dt_bias, ssd_a_log, ssd_d, ssd_norm_g, hyb_w_out,
              mla_w_in, mla_q_norm_g, mla_kv_norm_g, mla_w_uq, mla_w_ukv, mla_w_out):
    cos, sin = rope_tables(positions)
    h = x
    for i in range(DEPTH):
        j = i // 2
        f = swiglu(rmsnorm(h, ffn1_pre_g[i]), ffn1_w_in[i], ffn1_w_down[i])
        h = h + 0.5 * rmsnorm(f, ffn1_post_g[i])
        hn = rmsnorm(h, mix_pre_g[i])
        if i % 2 == 0:
            uv, zxbcdt = jnp.split(hn @ hyb_w_in[j], [2 * GM_WIDTH], axis=-1)
            ya = gmlp_spatial_gating(uv, gm_ln_g[j], gm_ln_b[j], gm_w_s[j], gm_b_s[j])
            yb = ssd_mixer(zxbcdt, ssd_conv_w[j], ssd_conv_b[j], ssd_dt_bias[j],
                           ssd_a_log[j], ssd_d[j], ssd_norm_g[j])
            mixed = jnp.concatenate([ya, yb], axis=-1) @ hyb_w_out[j]
        else:
            mixed = mla_attention(hn, mla_w_in[j], mla_q_norm_g[j], mla_kv_norm_g[j],
                                  mla_w_uq[j], mla_w_ukv[j], mla_w_out[j], cos, sin)
        h = h + rmsnorm(mixed, mix_post_g[i])
        f = swiglu(rmsnorm(h, ffn2_pre_g[i]), ffn2_w_in[i], ffn2_w_down[i])
        h = h + 0.5 * rmsnorm(f, ffn2_post_g[i])
        gate = jax.nn.sigmoid(rmsnorm(h, ple_pre_g[i]) @ ple_w_gate[i])
        h = h + rmsnorm(gate * (p[i] @ ple_w_proj[i]), ple_post_g[i])
    return h
```

```python
import functools
import math

import jax
import jax.numpy as jnp
from jax import lax
from jax.experimental import pallas as pl
from jax.experimental.pallas import tpu as pltpu

F32 = jnp.float32
BF16 = jnp.bfloat16

NORM_EPS = 1e-6
LN_EPS = 1e-5

LANES = 128
CHUNK = 128

GM_HEADS = 8
GM_HEAD_DIM = 128
GM_WIDTH = GM_HEADS * GM_HEAD_DIM

SSD_HEAD_DIM = 64
SSD_HEADS = 16
SSD_INNER = SSD_HEADS * SSD_HEAD_DIM
SSD_GROUPS = 2
SSD_STATE = 128
SSD_CONV = 4
SSD_CONV_CH = SSD_INNER + 2 * SSD_GROUPS * SSD_STATE
SSD_PAIRS = SSD_HEADS // 2
CONV_HALO = 8

MLA_HEADS = 16
MLA_NOPE = 128
MLA_ROPE = 64
MLA_V = 128
MLA_Q_LORA = 256
MLA_KV_LORA = 128
MLA_QK = MLA_NOPE + MLA_ROPE
MLA_QK_PAD = 256
ROPE_BASE = 10000.0

FFN_CHUNK = 256
VMEM_LIMIT = 56 * 1024 * 1024

TM_FFN = 512
TM_HYB = 512
TM_MLA = 512
TQ_ATTN = 512


def _rms(x, g):
    return x * lax.rsqrt(jnp.mean(x * x, axis=-1, keepdims=True) + NORM_EPS) * g


def _dot(a, b):
    return jnp.dot(a, b, preferred_element_type=F32)


def _const_spec(shape):
    nd = len(shape)
    return pl.BlockSpec(shape, lambda *_: (0,) * nd, pipeline_mode=pl.Buffered(1))


def _params(n_axes):
    return pltpu.CompilerParams(
        dimension_semantics=("arbitrary",) * n_axes, vmem_limit_bytes=VMEM_LIMIT)


def _ffn_body(h_ref, pre_g_ref, w_in_ref, w_down_ref, post_g_ref, o_ref, act_ref, *, d_ff):
    h = h_ref[...]
    xn = _rms(h, pre_g_ref[...]).astype(BF16)
    for c in range(d_ff // FFN_CHUNK):
        lo = c * FFN_CHUNK
        g = _dot(xn, w_in_ref[:, lo:lo + FFN_CHUNK])
        u = _dot(xn, w_in_ref[:, d_ff + lo:d_ff + lo + FFN_CHUNK])
        act_ref[:, lo:lo + FFN_CHUNK] = (g * jax.nn.sigmoid(g) * u).astype(BF16)
    f = _dot(act_ref[...], w_down_ref[...])
    o_ref[...] = h + 0.5 * _rms(f, post_g_ref[...])


def _ffn(h, pre_g, w_in, w_down, post_g):
    t, d = h.shape
    d_ff = w_down.shape[0]
    tm = min(TM_FFN, t)
    row = pl.BlockSpec((tm, d), lambda i: (i, 0))
    return pl.pallas_call(
        functools.partial(_ffn_body, d_ff=d_ff),
        grid=(t // tm,),
        in_specs=[row, _const_spec((1, d)), _const_spec((d, 2 * d_ff)),
                  _const_spec((d_ff, d)), _const_spec((1, d))],
        out_specs=row,
        out_shape=jax.ShapeDtypeStruct((t, d), F32),
        scratch_shapes=[pltpu.VMEM((tm, d_ff), BF16)],
        compiler_params=_params(1),
        name="ffn",
    )(h, pre_g, w_in, w_down, post_g)


def _ple_body(h_ref, p_ref, pre_g_ref, wg_ref, wp_ref, post_g_ref, o_ref):
    h = h_ref[...]
    xn = _rms(h, pre_g_ref[...]).astype(BF16)
    gate = jax.nn.sigmoid(_dot(xn, wg_ref[...]))
    proj = _dot(p_ref[...].astype(BF16), wp_ref[...])
    o_ref[...] = h + _rms(gate * proj, post_g_ref[...])


def _ple(h, p, pre_g, w_gate, w_proj, post_g):
    t, d = h.shape
    pd = p.shape[1]
    tm = min(TM_FFN, t)
    row = pl.BlockSpec((tm, d), lambda i: (i, 0))
    return pl.pallas_call(
        _ple_body,
        grid=(t // tm,),
        in_specs=[row, pl.BlockSpec((tm, pd), lambda i: (i, 0)), _const_spec((1, d)),
                  _const_spec((d, d)), _const_spec((pd, d)), _const_spec((1, d))],
        out_specs=row,
        out_shape=jax.ShapeDtypeStruct((t, d), F32),
        compiler_params=_params(1),
        name="ple",
    )(h, p, pre_g, w_gate, w_proj, post_g)


def _proj_res_body(h_ref, o_in_ref, w_ref, post_g_ref, o_ref):
    mixed = _dot(o_in_ref[...], w_ref[...])
    o_ref[...] = h_ref[...] + _rms(mixed, post_g_ref[...])


def _proj_res(h, o_in, w, post_g):
    t, d = h.shape
    k = o_in.shape[1]
    tm = min(TM_FFN, t)
    row = pl.BlockSpec((tm, d), lambda i: (i, 0))
    return pl.pallas_call(
        _proj_res_body,
        grid=(t // tm,),
        in_specs=[row, pl.BlockSpec((tm, k), lambda i: (i, 0)), _const_spec((k, d)),
                  _const_spec((1, d))],
        out_specs=row,
        out_shape=jax.ShapeDtypeStruct((t, d), F32),
        compiler_params=_params(1),
        name="proj_res",
    )(h, o_in, w, post_g)


def _mla_proj_body(h_ref, pos_ref, inv_ref, sgn_ref, pre_g_ref, w_in_ref, qg_ref, kvg_ref,
                   wq_ref, wkv_ref, q_ref, k_ref, v_ref, *, q_scale):
    ang = pos_ref[...].astype(F32) * inv_ref[...]
    cos = jnp.cos(ang)
    sin = jnp.sin(ang) * sgn_ref[...]
    hn = _rms(h_ref[...], pre_g_ref[...]).astype(BF16)
    c = _dot(hn, w_in_ref[...])
    cq = _rms(c[:, :MLA_Q_LORA], qg_ref[...]).astype(BF16)
    ckv = _rms(c[:, MLA_Q_LORA:MLA_Q_LORA + MLA_KV_LORA], kvg_ref[...]).astype(BF16)
    o = MLA_Q_LORA + MLA_KV_LORA
    kr = (c[:, o:o + LANES] * cos + c[:, o + LANES:o + 2 * LANES] * sin).astype(BF16)
    for hd in range(MLA_HEADS):
        qh = _dot(cq, wq_ref[hd])
        qn = qh[:, :MLA_NOPE]
        qr = qh[:, MLA_NOPE:MLA_NOPE + LANES] * cos + qh[:, MLA_NOPE + LANES:] * sin
        q_ref[:, hd * MLA_QK_PAD:hd * MLA_QK_PAD + MLA_NOPE] = (qn * q_scale).astype(BF16)
        q_ref[:, hd * MLA_QK_PAD + MLA_NOPE:(hd + 1) * MLA_QK_PAD] = (qr * q_scale).astype(BF16)
        kvh = _dot(ckv, wkv_ref[:, hd * (MLA_NOPE + MLA_V):(hd + 1) * (MLA_NOPE + MLA_V)])
        k_ref[:, hd * MLA_QK_PAD:hd * MLA_QK_PAD + MLA_NOPE] = kvh[:, :MLA_NOPE].astype(BF16)
        k_ref[:, hd * MLA_QK_PAD + MLA_NOPE:(hd + 1) * MLA_QK_PAD] = kr
        v_ref[:, hd * MLA_V:(hd + 1) * MLA_V] = kvh[:, MLA_NOPE:].astype(BF16)


def _mla_proj(h, pos, pre_g, w_in, q_norm_g, kv_norm_g, w_uq, w_ukv):
    t, d = h.shape
    tm = min(TM_MLA, t)
    half = MLA_ROPE // 2
    inv = 1.0 / (ROPE_BASE ** (jnp.arange(0, MLA_ROPE, 2, dtype=F32) / MLA_ROPE))
    zeros = jnp.zeros((LANES - MLA_ROPE,), F32)
    inv_slab = jnp.concatenate([inv, inv, zeros]).reshape(1, LANES)
    sgn_slab = jnp.concatenate([-jnp.ones((half,), F32), jnp.ones((half,), F32), zeros]).reshape(1, LANES)

    def rope_slabs(w):
        x1, x2 = w[..., :half], w[..., half:]
        z = jnp.zeros(w.shape[:-1] + (LANES - MLA_ROPE,), w.dtype)
        return jnp.concatenate([x1, x2, z, x2, x1, z], axis=-1)

    o = MLA_Q_LORA + MLA_KV_LORA
    w_in_k = jnp.concatenate([w_in[:, :o], rope_slabs(w_in[:, o:])], axis=1).astype(BF16)
    wq = w_uq.reshape(MLA_Q_LORA, MLA_HEADS, MLA_QK)
    wq = jnp.concatenate([wq[..., :MLA_NOPE], rope_slabs(wq[..., MLA_NOPE:])], axis=-1)
    wq = jnp.transpose(wq, (1, 0, 2)).astype(BF16)
    wkv = w_ukv.astype(BF16)
    q_scale = (MLA_QK ** -0.5) * math.log2(math.e)

    row = lambda w: pl.BlockSpec((tm, w), lambda i: (i, 0))
    return pl.pallas_call(
        functools.partial(_mla_proj_body, q_scale=q_scale),
        grid=(t // tm,),
        in_specs=[row(d), row(1), _const_spec((1, LANES)), _const_spec((1, LANES)),
                  _const_spec((1, d)), _const_spec(w_in_k.shape), _const_spec((1, MLA_Q_LORA)),
                  _const_spec((1, MLA_KV_LORA)), _const_spec(wq.shape), _const_spec(wkv.shape)],
        out_specs=[row(MLA_HEADS * MLA_QK_PAD), row(MLA_HEADS * MLA_QK_PAD), row(MLA_HEADS * MLA_V)],
        out_shape=[jax.ShapeDtypeStruct((t, MLA_HEADS * MLA_QK_PAD), BF16),
                   jax.ShapeDtypeStruct((t, MLA_HEADS * MLA_QK_PAD), BF16),
                   jax.ShapeDtypeStruct((t, MLA_HEADS * MLA_V), BF16)],
        compiler_params=_params(1),
        name="mla_proj",
    )(h, pos, inv_slab, sgn_slab, pre_g, w_in_k, q_norm_g, kv_norm_g, wq, wkv)


def _attn_body(q_ref, k_ref, v_ref, o_ref, m_ref, l_ref, acc_ref, *, tq):
    i = pl.program_id(2)
    q = q_ref[...]
    m_ref[...] = jnp.full(m_ref.shape, -jnp.inf, F32)
    l_ref[...] = jnp.zeros(l_ref.shape, F32)
    acc_ref[...] = jnp.zeros(acc_ref.shape, F32)

    def block(j, masked):
        r0 = pl.multiple_of(j * tq, tq)
        k = k_ref[pl.ds(r0, tq), :]
        v = v_ref[pl.ds(r0, tq), :]
        s = lax.dot_general(q, k, (((1,), (1,)), ((), ())), preferred_element_type=F32)
        if masked:
            row = lax.broadcasted_iota(jnp.int32, (tq, tq), 0)
            col = lax.broadcasted_iota(jnp.int32, (tq, tq), 1)
            s = jnp.where(col <= row, s, -jnp.inf)
        m_prev = m_ref[...]
        m_new = jnp.maximum(m_prev, jnp.max(s, axis=-1, keepdims=True))
        alpha = jnp.exp2(m_prev - m_new)
        p = jnp.exp2(s - m_new)
        l_ref[...] = alpha * l_ref[...] + jnp.sum(p, axis=-1, keepdims=True)
        acc_ref[...] = alpha * acc_ref[...] + _dot(p.astype(BF16), v)
        m_ref[...] = m_new

    def body(j, carry):
        block(j, False)
        return carry

    lax.fori_loop(0, i, body, 0)
    block(i, True)
    o_ref[...] = (acc_ref[...] / l_ref[...]).astype(o_ref.dtype)


def _attn(q, k, v):
    b, s, _ = q.shape
    tq = min(TQ_ATTN, s)
    return pl.pallas_call(
        functools.partial(_attn_body, tq=tq),
        grid=(b, MLA_HEADS, s // tq),
        in_specs=[pl.BlockSpec((None, tq, MLA_QK_PAD), lambda bi, h, i: (bi, i, h)),
                  pl.BlockSpec((None, s, MLA_QK_PAD), lambda bi, h, i: (bi, 0, h)),
                  pl.BlockSpec((None, s, MLA_V), lambda bi, h, i: (bi, 0, h))],
        out_specs=pl.BlockSpec((None, tq, MLA_V), lambda bi, h, i: (bi, i, h)),
        out_shape=jax.ShapeDtypeStruct((b, s, MLA_HEADS * MLA_V), BF16),
        scratch_shapes=[pltpu.VMEM((tq, 1), F32), pltpu.VMEM((tq, 1), F32),
                        pltpu.VMEM((tq, MLA_V), F32)],
        compiler_params=_params(3),
        name="attn",
    )(q, k, v)


def _softplus(x):
    return jnp.maximum(x, 0.0) + jnp.log1p(jnp.exp(-jnp.abs(x)))


def _hyb_body(h_ref, pre_g_ref, w_in_ref, ln_g_ref, ln_b_ref, ws_ref, bs_ref,
              conv_w_ref, conv_b_ref, dt_bias_ref, a_log_ref, dskip_ref, ng_ref,
              w_out_ref, post_g_ref, o_ref,
              ext_ref, state_ref, y_ref, xs_ref, b_ref, c_ref, z_ref, dt_ref, da_ref, *, tm):
    nc = tm // CHUNK
    t_idx = pl.program_id(1)

    @pl.when(t_idx == 0)
    def _():
        ext_ref[0:CONV_HALO, :] = jnp.zeros((CONV_HALO, SSD_CONV_CH), F32)
        state_ref[...] = jnp.zeros(state_ref.shape, F32)

    h = h_ref[...]
    hn = _rms(h, pre_g_ref[...]).astype(BF16)
    row = lax.broadcasted_iota(jnp.int32, (CHUNK, CHUNK), 0)
    col = lax.broadcasted_iota(jnp.int32, (CHUNK, CHUNK), 1)
    causal = col <= row

    u = jax.nn.gelu(_dot(hn, w_in_ref[:, 0:GM_WIDTH]))
    v = jax.nn.gelu(_dot(hn, w_in_ref[:, GM_WIDTH:2 * GM_WIDTH]))
    for hd in range(GM_HEADS):
        sl = slice(hd * GM_HEAD_DIM, (hd + 1) * GM_HEAD_DIM)
        vh = v[:, sl]
        mu = jnp.mean(vh, axis=-1, keepdims=True)
        xc = vh - mu
        var = jnp.mean(xc * xc, axis=-1, keepdims=True)
        vn = (xc * lax.rsqrt(var + LN_EPS) * ln_g_ref[:, sl] + ln_b_ref[:, sl]).astype(BF16)
        rhs = jnp.concatenate([vn[c * CHUNK:(c + 1) * CHUNK] for c in range(nc)], axis=1)
        w = jnp.where(causal, ws_ref[hd], 0.0).astype(BF16)
        mixed = _dot(w, rhs)
        bias = bs_ref[:, hd:hd + 1]
        for c in range(nc):
            y_ref[c * CHUNK:(c + 1) * CHUNK, sl] = (
                u[c * CHUNK:(c + 1) * CHUNK, sl] * (mixed[:, c * CHUNK:(c + 1) * CHUNK] + bias)
            ).astype(BF16)

    o = 2 * GM_WIDTH
    z_ref[...] = _dot(hn, w_in_ref[:, o:o + SSD_INNER])
    o += SSD_INNER
    ext_ref[CONV_HALO:CONV_HALO + tm, :] = _dot(hn, w_in_ref[:, o:o + SSD_CONV_CH])
    o += SSD_CONV_CH
    dt = _softplus(_dot(hn, w_in_ref[:, o:o + LANES]) + dt_bias_ref[...])
    dt_ref[...] = dt
    da_ref[...] = dt * (-jnp.exp(a_log_ref[...]))

    conv = conv_b_ref[...]
    for kk in range(SSD_CONV):
        lo = CONV_HALO - (SSD_CONV - 1) + kk
        conv = conv + conv_w_ref[kk:kk + 1, :] * ext_ref[lo:lo + tm, :]
    ext_ref[0:CONV_HALO, :] = ext_ref[tm:tm + CONV_HALO, :]
    xbc = conv * jax.nn.sigmoid(conv)
    xs_ref[...] = xbc[:, :SSD_INNER]
    b_ref[...] = xbc[:, SSD_INNER:SSD_INNER + SSD_GROUPS * SSD_STATE]
    c_ref[...] = xbc[:, SSD_INNER + SSD_GROUPS * SSD_STATE:]

    tril = jnp.where(causal, 1.0, 0.0).astype(F32)
    lane = lax.broadcasted_iota(jnp.int32, (CHUNK, LANES), 1)
    first_head = lane < SSD_HEAD_DIM

    def chunk_body(c, carry):
        r0 = pl.multiple_of(c * CHUNK, CHUNK)
        rows = pl.ds(r0, CHUNK)
        dt_c = dt_ref[rows, :]
        acs = jnp.dot(tril, da_ref[rows, :], precision=lax.Precision.HIGHEST,
                      preferred_element_type=F32)
        acs_t = acs.T
        bc = b_ref[rows, :]
        cc = c_ref[rows, :]
        for g in range(SSD_GROUPS):
            gs = slice(g * SSD_STATE, (g + 1) * SSD_STATE)
            bg_t = bc[:, gs].T.astype(BF16)
            cg = cc[:, gs].astype(BF16)
            cb = _dot(cg, bg_t)
            ys = []
            for pj in range(SSD_PAIRS // SSD_GROUPS):
                j = g * (SSD_PAIRS // SSD_GROUPS) + pj
                ha, hb = 2 * j, 2 * j + 1
                sl = slice(j * LANES, (j + 1) * LANES)
                acol_a = jnp.broadcast_to(acs[:, ha:ha + 1], (CHUNK, LANES))
                acol_b = jnp.broadcast_to(acs[:, hb:hb + 1], (CHUNK, LANES))
                arow_a = jnp.broadcast_to(acs_t[ha:ha + 1, :], (CHUNK, CHUNK))
                arow_b = jnp.broadcast_to(acs_t[hb:hb + 1, :], (CHUNK, CHUNK))
                dtw = jnp.where(first_head, jnp.broadcast_to(dt_c[:, ha:ha + 1], (CHUNK, LANES)),
                                jnp.broadcast_to(dt_c[:, hb:hb + 1], (CHUNK, LANES)))
                acw = jnp.where(first_head, acol_a, acol_b)
                xs_p = xs_ref[rows, sl]
                xd = xs_p * dtw
                xd_bf = xd.astype(BF16)
                m_a = (cb * jnp.exp(jnp.where(causal, acol_a - arow_a, -jnp.inf))).astype(BF16)
                m_b = (cb * jnp.exp(jnp.where(causal, acol_b - arow_b, -jnp.inf))).astype(BF16)
                y_diag = jnp.where(first_head, _dot(m_a, xd_bf), _dot(m_b, xd_bf))
                acw_last = acw[CHUNK - 1:CHUNK, :]
                xw = (xd * jnp.exp(acw_last - acw)).astype(BF16)
                st_new = _dot(bg_t, xw)
                prev = state_ref[:, sl]
                y_off = _dot(cg, prev.astype(BF16)) * jnp.exp(acw)
                state_ref[:, sl] = prev * jnp.exp(acw_last) + st_new
                zz = z_ref[rows, sl]
                y = y_diag + y_off + dskip_ref[:, sl] * xs_p
                ys.append(y * (zz * jax.nn.sigmoid(zz)))
            yg = jnp.concatenate(ys, axis=1)
            gw = SSD_INNER // SSD_GROUPS
            yn = _rms(yg, ng_ref[:, g * gw:(g + 1) * gw])
            y_ref[rows, GM_WIDTH + g * gw:GM_WIDTH + (g + 1) * gw] = yn.astype(BF16)
        return carry

    lax.fori_loop(0, nc, chunk_body, 0)

    mixed = _dot(y_ref[...], w_out_ref[...])
    o_ref[...] = h + _rms(mixed, post_g_ref[...])


def _hybrid(h, pre_g, w_in, ln_g, ln_b, w_s, b_s, conv_w, conv_b, dt_bias, a_log, d_skip,
            norm_g, w_out, post_g):
    b, s, d = h.shape
    tm = min(TM_HYB, s)
    hyb_in = w_in.shape[1]
    pad = (-hyb_in) % LANES
    w_in_k = jnp.pad(w_in, ((0, 0), (0, pad))).astype(BF16)
    pad_h = lambda a: jnp.pad(a.reshape(1, SSD_HEADS), ((0, 0), (0, LANES - SSD_HEADS)))
    row = pl.BlockSpec((None, tm, d), lambda bi, i: (bi, i, 0))
    args = (h, pre_g, w_in_k, ln_g.reshape(1, GM_WIDTH), ln_b.reshape(1, GM_WIDTH), w_s,
            b_s.T, conv_w, conv_b.reshape(1, SSD_CONV_CH), pad_h(dt_bias), pad_h(a_log),
            jnp.repeat(d_skip, SSD_HEAD_DIM).reshape(1, SSD_INNER), norm_g.reshape(1, SSD_INNER),
            w_out.astype(BF16), post_g)
    return pl.pallas_call(
        functools.partial(_hyb_body, tm=tm),
        grid=(b, s // tm),
        in_specs=[row] + [_const_spec(a.shape) for a in args[1:]],
        out_specs=row,
        out_shape=jax.ShapeDtypeStruct((b, s, d), F32),
        scratch_shapes=[
            pltpu.VMEM((tm + CONV_HALO, SSD_CONV_CH), F32),
            pltpu.VMEM((SSD_STATE, SSD_INNER), F32),
            pltpu.VMEM((tm, GM_WIDTH + SSD_INNER), BF16),
            pltpu.VMEM((tm, SSD_INNER), F32),
            pltpu.VMEM((tm, SSD_GROUPS * SSD_STATE), F32),
            pltpu.VMEM((tm, SSD_GROUPS * SSD_STATE), F32),
            pltpu.VMEM((tm, SSD_INNER), F32),
            pltpu.VMEM((tm, LANES), F32),
            pltpu.VMEM((tm, LANES), F32),
        ],
        compiler_params=_params(2),
        name="hybrid",
    )(*args)


def kernel(x, p, positions, ffn1_pre_g, ffn1_w_in, ffn1_w_down, ffn1_post_g, mix_pre_g, mix_post_g, ffn2_pre_g, ffn2_w_in, ffn2_w_down, ffn2_post_g, ple_pre_g, ple_w_gate, ple_w_proj, ple_post_g, hyb_w_in, gm_ln_g, gm_ln_b, gm_w_s, gm_b_s, ssd_conv_w, ssd_conv_b, ssd_dt_bias, ssd_a_log, ssd_d, ssd_norm_g, hyb_w_out, mla_w_in, mla_q_norm_g, mla_kv_norm_g, mla_w_uq, mla_w_ukv, mla_w_out):
    b, s, d = x.shape
    t = b * s
    depth = ffn1_w_in.shape[0]
    g2 = lambda a: a.reshape(1, -1)
    pos = positions.reshape(t, 1)
    h = x.reshape(t, d)
    for i in range(depth):
        j = i // 2
        h = _ffn(h, g2(ffn1_pre_g[i]), ffn1_w_in[i].astype(BF16), ffn1_w_down[i].astype(BF16),
                 g2(ffn1_post_g[i]))
        if i % 2 == 0:
            h = _hybrid(h.reshape(b, s, d), g2(mix_pre_g[i]), hyb_w_in[j], gm_ln_g[j], gm_ln_b[j],
                        gm_w_s[j], gm_b_s[j], ssd_conv_w[j], ssd_conv_b[j], ssd_dt_bias[j],
                        ssd_a_log[j], ssd_d[j], ssd_norm_g[j], hyb_w_out[j],
                        g2(mix_post_g[i])).reshape(t, d)
        else:
            q, k, v = _mla_proj(h, pos, g2(mix_pre_g[i]), mla_w_in[j], g2(mla_q_norm_g[j]),
                                g2(mla_kv_norm_g[j]), mla_w_uq[j], mla_w_ukv[j])
            o = _attn(q.reshape(b, s, -1), k.reshape(b, s, -1), v.reshape(b, s, -1))
            h = _proj_res(h, o.reshape(t, -1), mla_w_out[j].astype(BF16), g2(mix_post_g[i]))
        h = _ffn(h, g2(ffn2_pre_g[i]), ffn2_w_in[i].astype(BF16), ffn2_w_down[i].astype(BF16),
                 g2(ffn2_post_g[i]))
        h = _ple(h, p[i].reshape(t, -1), g2(ple_pre_g[i]), ple_w_gate[i].astype(BF16),
                 ple_w_proj[i].astype(BF16), g2(ple_post_g[i]))
    return h.reshape(b, s, d)
```

```python
import functools
import math

import jax
import jax.numpy as jnp
from jax import lax
from jax.experimental import pallas as pl
from jax.experimental.pallas import tpu as pltpu

F32 = jnp.float32
BF16 = jnp.bfloat16

NORM_EPS = 1e-6
LN_EPS = 1e-5

LANES = 128
CHUNK = 128

GM_HEADS = 8
GM_HEAD_DIM = 128
GM_WIDTH = GM_HEADS * GM_HEAD_DIM

SSD_HEAD_DIM = 64
SSD_HEADS = 16
SSD_INNER = SSD_HEADS * SSD_HEAD_DIM
SSD_GROUPS = 2
SSD_STATE = 128
SSD_CONV = 4
SSD_CONV_CH = SSD_INNER + 2 * SSD_GROUPS * SSD_STATE
SSD_PAIRS = SSD_HEADS // 2
CONV_HALO = 8

MLA_HEADS = 16
MLA_NOPE = 128
MLA_ROPE = 64
MLA_V = 128
MLA_Q_LORA = 256
MLA_KV_LORA = 128
MLA_QK = MLA_NOPE + MLA_ROPE
MLA_QK_PAD = 256
ROPE_BASE = 10000.0

FFN_CHUNK = 256
VMEM_LIMIT = 56 * 1024 * 1024

TM_FFN = 512
TM_HYB = 512
QC_ATTN = 512
TQ_ATTN = 2048
TK_ATTN = 1024


def _rms(x, g):
    return x * lax.rsqrt(jnp.mean(x * x, axis=-1, keepdims=True) + NORM_EPS) * g


def _dot(a, b):
    return jnp.dot(a, b, preferred_element_type=F32)


def _const_spec(shape):
    nd = len(shape)
    return pl.BlockSpec(shape, lambda *_: (0,) * nd, pipeline_mode=pl.Buffered(1))


def _params(n_axes):
    return pltpu.CompilerParams(
        dimension_semantics=("arbitrary",) * n_axes, vmem_limit_bytes=VMEM_LIMIT)


def _ffn_body(h_ref, pre_g_ref, w_in_ref, w_down_ref, post_g_ref, o_ref, act_ref, *, d_ff):
    h = h_ref[...]
    xn = _rms(h, pre_g_ref[...]).astype(BF16)
    for c in range(d_ff // FFN_CHUNK):
        lo = c * FFN_CHUNK
        g = _dot(xn, w_in_ref[:, lo:lo + FFN_CHUNK])
        u = _dot(xn, w_in_ref[:, d_ff + lo:d_ff + lo + FFN_CHUNK])
        act_ref[:, lo:lo + FFN_CHUNK] = (g * jax.nn.sigmoid(g) * u).astype(BF16)
    f = _dot(act_ref[...], w_down_ref[...])
    o_ref[...] = h + 0.5 * _rms(f, post_g_ref[...])


def _ffn(h, pre_g, w_in, w_down, post_g):
    t, d = h.shape
    d_ff = w_down.shape[0]
    tm = min(TM_FFN, t)
    row = pl.BlockSpec((tm, d), lambda i: (i, 0))
    return pl.pallas_call(
        functools.partial(_ffn_body, d_ff=d_ff),
        grid=(t // tm,),
        in_specs=[row, _const_spec((1, d)), _const_spec((d, 2 * d_ff)),
                  _const_spec((d_ff, d)), _const_spec((1, d))],
        out_specs=row,
        out_shape=jax.ShapeDtypeStruct((t, d), F32),
        scratch_shapes=[pltpu.VMEM((tm, d_ff), BF16)],
        compiler_params=_params(1),
        name="ffn",
    )(h, pre_g, w_in, w_down, post_g)


def _ple_body(h_ref, p_ref, pre_g_ref, wg_ref, wp_ref, post_g_ref, o_ref):
    h = h_ref[...]
    xn = _rms(h, pre_g_ref[...]).astype(BF16)
    gate = jax.nn.sigmoid(_dot(xn, wg_ref[...]))
    proj = _dot(p_ref[...].astype(BF16), wp_ref[...])
    o_ref[...] = h + _rms(gate * proj, post_g_ref[...])


def _ple(h, p, pre_g, w_gate, w_proj, post_g):
    t, d = h.shape
    pd = p.shape[1]
    tm = min(TM_FFN, t)
    row = pl.BlockSpec((tm, d), lambda i: (i, 0))
    return pl.pallas_call(
        _ple_body,
        grid=(t // tm,),
        in_specs=[row, pl.BlockSpec((tm, pd), lambda i: (i, 0)), _const_spec((1, d)),
                  _const_spec((d, d)), _const_spec((pd, d)), _const_spec((1, d))],
        out_specs=row,
        out_shape=jax.ShapeDtypeStruct((t, d), F32),
        compiler_params=_params(1),
        name="ple",
    )(h, p, pre_g, w_gate, w_proj, post_g)


def _proj_res_body(h_ref, o_in_ref, w_ref, post_g_ref, o_ref):
    mixed = _dot(o_in_ref[...], w_ref[...])
    o_ref[...] = h_ref[...] + _rms(mixed, post_g_ref[...])


def _proj_res(h, o_in, w, post_g):
    t, d = h.shape
    k = o_in.shape[1]
    tm = min(TM_FFN, t)
    row = pl.BlockSpec((tm, d), lambda i: (i, 0))
    return pl.pallas_call(
        _proj_res_body,
        grid=(t // tm,),
        in_specs=[row, pl.BlockSpec((tm, k), lambda i: (i, 0)), _const_spec((k, d)),
                  _const_spec((1, d))],
        out_specs=row,
        out_shape=jax.ShapeDtypeStruct((t, d), F32),
        compiler_params=_params(1),
        name="proj_res",
    )(h, o_in, w, post_g)


def _mla_proj_body(h_ref, pos_ref, inv_ref, pre_g_ref, w_in_ref, qg_ref, kvg_ref,
                   wq_t_ref, wk_ref, wv_t_ref, q_t_ref, k_ref, v_t_ref, *, q_scale):
    tm = h_ref.shape[0]
    ang = inv_ref[...] * pos_ref[...].astype(F32)
    cos32 = jnp.cos(ang)
    sin32 = jnp.sin(ang)
    pad = jnp.zeros((LANES - MLA_ROPE, tm), F32)
    cos_t = jnp.concatenate([cos32, cos32, pad], axis=0)
    sin_t = jnp.concatenate([-sin32, sin32, pad], axis=0)
    cos_n = cos_t.T
    sin_n = sin_t.T
    hn = _rms(h_ref[...], pre_g_ref[...]).astype(BF16)
    c = _dot(hn, w_in_ref[...])
    cq = _rms(c[:, :MLA_Q_LORA], qg_ref[...])
    ckv = _rms(c[:, MLA_Q_LORA:MLA_Q_LORA + MLA_KV_LORA], kvg_ref[...])
    o = MLA_Q_LORA + MLA_KV_LORA
    kr = (c[:, o:o + LANES] * cos_n + c[:, o + LANES:o + 2 * LANES] * sin_n).astype(BF16)
    cq_t = cq.T.astype(BF16)
    ckv_t = ckv.T.astype(BF16)
    kn = _dot(ckv.astype(BF16), wk_ref[...])
    v_t = _dot(wv_t_ref[...], ckv_t)
    for hd in range(MLA_HEADS):
        k_ref[:, hd * MLA_QK_PAD:hd * MLA_QK_PAD + MLA_NOPE] = (
            kn[:, hd * MLA_NOPE:(hd + 1) * MLA_NOPE].astype(BF16))
        k_ref[:, hd * MLA_QK_PAD + MLA_NOPE:(hd + 1) * MLA_QK_PAD] = kr
        v_t_ref[hd] = v_t[hd * MLA_V:(hd + 1) * MLA_V, :].astype(BF16)
        qh = _dot(wq_t_ref[hd], cq_t)
        qr = qh[MLA_NOPE:MLA_NOPE + LANES] * cos_t + qh[MLA_NOPE + LANES:] * sin_t
        q_t_ref[hd, 0:MLA_NOPE, :] = (qh[:MLA_NOPE] * q_scale).astype(BF16)
        q_t_ref[hd, MLA_NOPE:MLA_QK_PAD, :] = (qr * q_scale).astype(BF16)


def _mla_proj(h, pos, pre_g, w_in, q_norm_g, kv_norm_g, w_uq, w_ukv, qc):
    t, d = h.shape
    half = MLA_ROPE // 2
    inv = 1.0 / (ROPE_BASE ** (jnp.arange(0, MLA_ROPE, 2, dtype=F32) / MLA_ROPE))

    def rope_slabs(w):
        x1, x2 = w[..., :half], w[..., half:]
        z = jnp.zeros(w.shape[:-1] + (LANES - MLA_ROPE,), w.dtype)
        return jnp.concatenate([x1, x2, z, x2, x1, z], axis=-1)

    o = MLA_Q_LORA + MLA_KV_LORA
    w_in_k = jnp.concatenate([w_in[:, :o], rope_slabs(w_in[:, o:])], axis=1).astype(BF16)
    wq = w_uq.reshape(MLA_Q_LORA, MLA_HEADS, MLA_QK)
    wq = jnp.concatenate([wq[..., :MLA_NOPE], rope_slabs(wq[..., MLA_NOPE:])], axis=-1)
    wq_t = jnp.transpose(wq, (1, 2, 0)).astype(BF16)
    wkv = w_ukv.reshape(MLA_KV_LORA, MLA_HEADS, MLA_NOPE + MLA_V)
    wk = wkv[..., :MLA_NOPE].reshape(MLA_KV_LORA, MLA_HEADS * MLA_NOPE).astype(BF16)
    wv_t = wkv[..., MLA_NOPE:].reshape(MLA_KV_LORA, MLA_HEADS * MLA_V).T.astype(BF16)
    q_scale = (MLA_QK ** -0.5) * math.log2(math.e)

    n = t // qc
    return pl.pallas_call(
        functools.partial(_mla_proj_body, q_scale=q_scale),
        grid=(n,),
        in_specs=[pl.BlockSpec((qc, d), lambda i: (i, 0)), pl.BlockSpec((1, qc), lambda i: (0, i)),
                  _const_spec((half, 1)), _const_spec((1, d)), _const_spec(w_in_k.shape),
                  _const_spec((1, MLA_Q_LORA)), _const_spec((1, MLA_KV_LORA)),
                  _const_spec(wq_t.shape), _const_spec(wk.shape), _const_spec(wv_t.shape)],
        out_specs=[pl.BlockSpec((None, MLA_HEADS, MLA_QK_PAD, qc), lambda i: (i, 0, 0, 0)),
                   pl.BlockSpec((qc, MLA_HEADS * MLA_QK_PAD), lambda i: (i, 0)),
                   pl.BlockSpec((None, MLA_HEADS, MLA_V, qc), lambda i: (i, 0, 0, 0))],
        out_shape=[jax.ShapeDtypeStruct((n, MLA_HEADS, MLA_QK_PAD, qc), BF16),
                   jax.ShapeDtypeStruct((t, MLA_HEADS * MLA_QK_PAD), BF16),
                   jax.ShapeDtypeStruct((n, MLA_HEADS, MLA_V, qc), BF16)],
        compiler_params=_params(1),
        name="mla_proj",
    )(h, pos.reshape(1, t), inv.reshape(half, 1), pre_g, w_in_k, q_norm_g, kv_norm_g, wq_t, wk, wv_t)


def _qk_t(k, q_t, mask):
    s_t = _dot(k, q_t)
    if mask is not None:
        s_t = jnp.where(mask, s_t, -jnp.inf)
    return s_t


def _softmax_pv_t(s_t, v_t, m_ref, l_ref, acc_ref, r):
    m_prev = m_ref[r]
    m_new = jnp.maximum(m_prev, jnp.max(s_t, axis=0, keepdims=True))
    alpha = jnp.exp2(m_prev - m_new)
    p_t = jnp.exp2(s_t - m_new)
    l_ref[r] = alpha * l_ref[r] + jnp.sum(p_t, axis=0, keepdims=True)
    acc_ref[r] = _dot(v_t, p_t.astype(BF16)) + alpha * acc_ref[r]
    m_ref[r] = m_new


def _attn_body(q_t_ref, k_ref, v_t_ref, o_ref, m_ref, l_ref, acc_ref, *, tq, tk, qc):
    i = pl.program_id(2)
    n_sub = tq // qc
    kc = tk // qc
    m_ref[...] = jnp.full(m_ref.shape, -jnp.inf, F32)
    l_ref[...] = jnp.zeros(l_ref.shape, F32)
    acc_ref[...] = jnp.zeros(acc_ref.shape, F32)

    def body(j, carry):
        r0 = pl.multiple_of(j * tk, tk)
        k = k_ref[pl.ds(r0, tk), :]
        v_t = jnp.concatenate([v_t_ref[j * kc + c] for c in range(kc)], axis=1)
        s_t = _qk_t(k, q_t_ref[0], None)
        for r in range(n_sub):
            s_next = _qk_t(k, q_t_ref[r + 1], None) if r + 1 < n_sub else None
            _softmax_pv_t(s_t, v_t, m_ref, l_ref, acc_ref, r)
            s_t = s_next
        return carry

    lax.fori_loop(0, i * (tq // tk), body, 0)

    d0 = pl.multiple_of(i * tq, tq)
    krow = lax.broadcasted_iota(jnp.int32, (qc, qc), 0)
    qcol = lax.broadcasted_iota(jnp.int32, (qc, qc), 1)
    for r in range(n_sub):
        mask = krow <= qcol
        if r:
            mask = jnp.concatenate([jnp.ones((r * qc, qc), jnp.bool_), mask], axis=0)
        v_t = jnp.concatenate([v_t_ref[i * n_sub + c] for c in range(r + 1)], axis=1)
        s_t = _qk_t(k_ref[pl.ds(d0, (r + 1) * qc), :], q_t_ref[r], mask)
        _softmax_pv_t(s_t, v_t, m_ref, l_ref, acc_ref, r)

    for r in range(n_sub):
        o_t = acc_ref[r] / l_ref[r]
        o_ref[r * qc:(r + 1) * qc, :] = o_t.T.astype(o_ref.dtype)


def _attn(q_t, k, v_t, b, s, qc):
    tq = min(TQ_ATTN, s)
    tk = min(TK_ATTN, tq)
    n_sub = tq // qc
    return pl.pallas_call(
        functools.partial(_attn_body, tq=tq, tk=tk, qc=qc),
        grid=(b, MLA_HEADS, s // tq),
        in_specs=[pl.BlockSpec((n_sub, None, MLA_QK_PAD, qc),
                               lambda bi, h, i: (bi * (s // tq) + i, h, 0, 0)),
                  pl.BlockSpec((None, s, MLA_QK_PAD), lambda bi, h, i: (bi, 0, h)),
                  pl.BlockSpec((s // qc, None, MLA_V, qc), lambda bi, h, i: (bi, h, 0, 0))],
        out_specs=pl.BlockSpec((None, tq, MLA_V), lambda bi, h, i: (bi, i, h)),
        out_shape=jax.ShapeDtypeStruct((b, s, MLA_HEADS * MLA_V), BF16),
        scratch_shapes=[pltpu.VMEM((n_sub, 1, qc), F32), pltpu.VMEM((n_sub, 1, qc), F32),
                        pltpu.VMEM((n_sub, MLA_V, qc), F32)],
        compiler_params=_params(3),
        name="attn",
    )(q_t, k.reshape(b, s, MLA_HEADS * MLA_QK_PAD), v_t)


def _softplus(x):
    return jnp.maximum(x, 0.0) + jnp.log1p(jnp.exp(-jnp.abs(x)))


def _hyb_body(h_ref, pre_g_ref, w_in_ref, ln_g_ref, ln_b_ref, ws_ref, bs_ref,
              conv_w_ref, conv_b_ref, dt_bias_ref, a_log_ref, dskip_ref, ng_ref,
              w_out_ref, post_g_ref, o_ref,
              ext_ref, state_ref, y_ref, xs_ref, b_ref, c_ref, z_ref, dt_ref, da_ref, *, tm):
    nc = tm // CHUNK
    t_idx = pl.program_id(1)

    @pl.when(t_idx == 0)
    def _():
        ext_ref[0:CONV_HALO, :] = jnp.zeros((CONV_HALO, SSD_CONV_CH), F32)
        state_ref[...] = jnp.zeros(state_ref.shape, F32)

    h = h_ref[...]
    hn = _rms(h, pre_g_ref[...]).astype(BF16)
    row = lax.broadcasted_iota(jnp.int32, (CHUNK, CHUNK), 0)
    col = lax.broadcasted_iota(jnp.int32, (CHUNK, CHUNK), 1)
    causal = col <= row

    u = jax.nn.gelu(_dot(hn, w_in_ref[:, 0:GM_WIDTH]))
    v = jax.nn.gelu(_dot(hn, w_in_ref[:, GM_WIDTH:2 * GM_WIDTH]))
    for hd in range(GM_HEADS):
        sl = slice(hd * GM_HEAD_DIM, (hd + 1) * GM_HEAD_DIM)
        vh = v[:, sl]
        mu = jnp.mean(vh, axis=-1, keepdims=True)
        xc = vh - mu
        var = jnp.mean(xc * xc, axis=-1, keepdims=True)
        vn = (xc * lax.rsqrt(var + LN_EPS) * ln_g_ref[:, sl] + ln_b_ref[:, sl]).astype(BF16)
        rhs = jnp.concatenate([vn[c * CHUNK:(c + 1) * CHUNK] for c in range(nc)], axis=1)
        w = jnp.where(causal, ws_ref[hd], 0.0).astype(BF16)
        mixed = _dot(w, rhs)
        bias = bs_ref[:, hd:hd + 1]
        for c in range(nc):
            y_ref[c * CHUNK:(c + 1) * CHUNK, sl] = (
                u[c * CHUNK:(c + 1) * CHUNK, sl] * (mixed[:, c * CHUNK:(c + 1) * CHUNK] + bias)
            ).astype(BF16)

    o = 2 * GM_WIDTH
    z_ref[...] = _dot(hn, w_in_ref[:, o:o + SSD_INNER])
    o += SSD_INNER
    ext_ref[CONV_HALO:CONV_HALO + tm, :] = _dot(hn, w_in_ref[:, o:o + SSD_CONV_CH])
    o += SSD_CONV_CH
    dt = _softplus(_dot(hn, w_in_ref[:, o:o + LANES]) + dt_bias_ref[...])
    dt_ref[...] = dt
    da_ref[...] = dt * (-jnp.exp(a_log_ref[...]))

    conv = conv_b_ref[...]
    for kk in range(SSD_CONV):
        lo = CONV_HALO - (SSD_CONV - 1) + kk
        conv = conv + conv_w_ref[kk:kk + 1, :] * ext_ref[lo:lo + tm, :]
    ext_ref[0:CONV_HALO, :] = ext_ref[tm:tm + CONV_HALO, :]
    xbc = conv * jax.nn.sigmoid(conv)
    xs_ref[...] = xbc[:, :SSD_INNER]
    b_ref[...] = xbc[:, SSD_INNER:SSD_INNER + SSD_GROUPS * SSD_STATE]
    c_ref[...] = xbc[:, SSD_INNER + SSD_GROUPS * SSD_STATE:]

    tril = jnp.where(causal, 1.0, 0.0).astype(F32)
    lane = lax.broadcasted_iota(jnp.int32, (CHUNK, LANES), 1)
    first_head = lane < SSD_HEAD_DIM

    def chunk_body(c, carry):
        r0 = pl.multiple_of(c * CHUNK, CHUNK)
        rows = pl.ds(r0, CHUNK)
        dt_c = dt_ref[rows, :]
        acs = jnp.dot(tril, da_ref[rows, :], precision=lax.Precision.HIGHEST,
                      preferred_element_type=F32)
        acs_t = acs.T
        bc = b_ref[rows, :]
        cc = c_ref[rows, :]
        for g in range(SSD_GROUPS):
            gs = slice(g * SSD_STATE, (g + 1) * SSD_STATE)
            bg_t = bc[:, gs].T.astype(BF16)
            cg = cc[:, gs].astype(BF16)
            cb = _dot(cg, bg_t)
            ys = []
            for pj in range(SSD_PAIRS // SSD_GROUPS):
                j = g * (SSD_PAIRS // SSD_GROUPS) + pj
                ha, hb = 2 * j, 2 * j + 1
                sl = slice(j * LANES, (j + 1) * LANES)
                acol_a = jnp.broadcast_to(acs[:, ha:ha + 1], (CHUNK, LANES))
                acol_b = jnp.broadcast_to(acs[:, hb:hb + 1], (CHUNK, LANES))
                arow_a = jnp.broadcast_to(acs_t[ha:ha + 1, :], (CHUNK, CHUNK))
                arow_b = jnp.broadcast_to(acs_t[hb:hb + 1, :], (CHUNK, CHUNK))
                dtw = jnp.where(first_head, jnp.broadcast_to(dt_c[:, ha:ha + 1], (CHUNK, LANES)),
                                jnp.broadcast_to(dt_c[:, hb:hb + 1], (CHUNK, LANES)))
                acw = jnp.where(first_head, acol_a, acol_b)
                xs_p = xs_ref[rows, sl]
                xd = xs_p * dtw
                xd_bf = xd.astype(BF16)
                m_a = (cb * jnp.exp(jnp.where(causal, acol_a - arow_a, -jnp.inf))).astype(BF16)
                m_b = (cb * jnp.exp(jnp.where(causal, acol_b - arow_b, -jnp.inf))).astype(BF16)
                y_diag = jnp.where(first_head, _dot(m_a, xd_bf), _dot(m_b, xd_bf))
                acw_last = acw[CHUNK - 1:CHUNK, :]
                xw = (xd * jnp.exp(acw_last - acw)).astype(BF16)
                st_new = _dot(bg_t, xw)
                prev = state_ref[:, sl]
                y_off = _dot(cg, prev.astype(BF16)) * jnp.exp(acw)
                state_ref[:, sl] = prev * jnp.exp(acw_last) + st_new
                zz = z_ref[rows, sl]
                y = y_diag + y_off + dskip_ref[:, sl] * xs_p
                ys.append(y * (zz * jax.nn.sigmoid(zz)))
            yg = jnp.concatenate(ys, axis=1)
            gw = SSD_INNER // SSD_GROUPS
            yn = _rms(yg, ng_ref[:, g * gw:(g + 1) * gw])
            y_ref[rows, GM_WIDTH + g * gw:GM_WIDTH + (g + 1) * gw] = yn.astype(BF16)
        return carry

    lax.fori_loop(0, nc, chunk_body, 0)

    mixed = _dot(y_ref[...], w_out_ref[...])
    o_ref[...] = h + _rms(mixed, post_g_ref[...])


def _hybrid(h, pre_g, w_in, ln_g, ln_b, w_s, b_s, conv_w, conv_b, dt_bias, a_log, d_skip,
            norm_g, w_out, post_g):
    b, s, d = h.shape
    tm = min(TM_HYB, s)
    hyb_in = w_in.shape[1]
    pad = (-hyb_in) % LANES
    w_in_k = jnp.pad(w_in, ((0, 0), (0, pad))).astype(BF16)
    pad_h = lambda a: jnp.pad(a.reshape(1, SSD_HEADS), ((0, 0), (0, LANES - SSD_HEADS)))
    row = pl.BlockSpec((None, tm, d), lambda bi, i: (bi, i, 0))
    args = (h, pre_g, w_in_k, ln_g.reshape(1, GM_WIDTH), ln_b.reshape(1, GM_WIDTH), w_s,
            b_s.T, conv_w, conv_b.reshape(1, SSD_CONV_CH), pad_h(dt_bias), pad_h(a_log),
            jnp.repeat(d_skip, SSD_HEAD_DIM).reshape(1, SSD_INNER), norm_g.reshape(1, SSD_INNER),
            w_out.astype(BF16), post_g)
    return pl.pallas_call(
        functools.partial(_hyb_body, tm=tm),
        grid=(b, s // tm),
        in_specs=[row] + [_const_spec(a.shape) for a in args[1:]],
        out_specs=row,
        out_shape=jax.ShapeDtypeStruct((b, s, d), F32),
        scratch_shapes=[
            pltpu.VMEM((tm + CONV_HALO, SSD_CONV_CH), F32),
            pltpu.VMEM((SSD_STATE, SSD_INNER), F32),
            pltpu.VMEM((tm, GM_WIDTH + SSD_INNER), BF16),
            pltpu.VMEM((tm, SSD_INNER), F32),
            pltpu.VMEM((tm, SSD_GROUPS * SSD_STATE), F32),
            pltpu.VMEM((tm, SSD_GROUPS * SSD_STATE), F32),
            pltpu.VMEM((tm, SSD_INNER), F32),
            pltpu.VMEM((tm, LANES), F32),
            pltpu.VMEM((tm, LANES), F32),
        ],
        compiler_params=_params(2),
        name="hybrid",
    )(*args)


def kernel(x, p, positions, ffn1_pre_g, ffn1_w_in, ffn1_w_down, ffn1_post_g, mix_pre_g, mix_post_g, ffn2_pre_g, ffn2_w_in, ffn2_w_down, ffn2_post_g, ple_pre_g, ple_w_gate, ple_w_proj, ple_post_g, hyb_w_in, gm_ln_g, gm_ln_b, gm_w_s, gm_b_s, ssd_conv_w, ssd_conv_b, ssd_dt_bias, ssd_a_log, ssd_d, ssd_norm_g, hyb_w_out, mla_w_in, mla_q_norm_g, mla_kv_norm_g, mla_w_uq, mla_w_ukv, mla_w_out):
    b, s, d = x.shape
    t = b * s
    depth = ffn1_w_in.shape[0]
    g2 = lambda a: a.reshape(1, -1)
    h = x.reshape(t, d)
    for i in range(depth):
        j = i // 2
        h = _ffn(h, g2(ffn1_pre_g[i]), ffn1_w_in[i].astype(BF16), ffn1_w_down[i].astype(BF16),
                 g2(ffn1_post_g[i]))
        if i % 2 == 0:
            h = _hybrid(h.reshape(b, s, d), g2(mix_pre_g[i]), hyb_w_in[j], gm_ln_g[j], gm_ln_b[j],
                        gm_w_s[j], gm_b_s[j], ssd_conv_w[j], ssd_conv_b[j], ssd_dt_bias[j],
                        ssd_a_log[j], ssd_d[j], ssd_norm_g[j], hyb_w_out[j],
                        g2(mix_post_g[i])).reshape(t, d)
        else:
            qc = min(QC_ATTN, s)
            q_t, k, v_t = _mla_proj(h, positions, g2(mix_pre_g[i]), mla_w_in[j], g2(mla_q_norm_g[j]),
                                    g2(mla_kv_norm_g[j]), mla_w_uq[j], mla_w_ukv[j], qc)
            o = _attn(q_t, k, v_t, b, s, qc)
            h = _proj_res(h, o.reshape(t, -1), mla_w_out[j].astype(BF16), g2(mix_post_g[i]))
        h = _ffn(h, g2(ffn2_pre_g[i]), ffn2_w_in[i].astype(BF16), ffn2_w_down[i].astype(BF16),
                 g2(ffn2_post_g[i]))
        h = _ple(h, p[i].reshape(t, -1), g2(ple_pre_g[i]), ple_w_gate[i].astype(BF16),
                 ple_w_proj[i].astype(BF16), g2(ple_post_g[i]))
    return h.reshape(b, s, d)
```

```python
import functools
import math

import jax
import jax.numpy as jnp
from jax import lax
from jax.experimental import pallas as pl
from jax.experimental.pallas import tpu as pltpu

F32 = jnp.float32
BF16 = jnp.bfloat16

NORM_EPS = 1e-6
LN_EPS = 1e-5

LANES = 128
CHUNK = 128

GM_HEADS = 8
GM_HEAD_DIM = 128
GM_WIDTH = GM_HEADS * GM_HEAD_DIM

SSD_HEAD_DIM = 64
SSD_HEADS = 16
SSD_INNER = SSD_HEADS * SSD_HEAD_DIM
SSD_GROUPS = 2
SSD_STATE = 128
SSD_CONV = 4
SSD_CONV_CH = SSD_INNER + 2 * SSD_GROUPS * SSD_STATE
SSD_PAIRS = SSD_HEADS // 2
CONV_HALO = 8

MLA_HEADS = 16
MLA_NOPE = 128
MLA_ROPE = 64
MLA_V = 128
MLA_Q_LORA = 256
MLA_KV_LORA = 128
MLA_QK = MLA_NOPE + MLA_ROPE
MLA_QK_PAD = 256
ROPE_BASE = 10000.0

FFN_CHUNK = 256
VMEM_LIMIT = 56 * 1024 * 1024

TM_FFN = 512
TM_HYB = 512
QC_ATTN = 512
TQ_ATTN = 2048
TK_ATTN = 1024
KB_UNROLL = 2


def _rms(x, g):
    return x * lax.rsqrt(jnp.mean(x * x, axis=-1, keepdims=True) + NORM_EPS) * g


def _dot(a, b):
    return jnp.dot(a, b, preferred_element_type=F32)


def _const_spec(shape):
    nd = len(shape)
    return pl.BlockSpec(shape, lambda *_: (0,) * nd, pipeline_mode=pl.Buffered(1))


def _layer_spec(stacked, layer):
    zeros = (0,) * (stacked.ndim - 1)
    return pl.BlockSpec((None,) + stacked.shape[1:], lambda *_: (layer,) + zeros,
                        pipeline_mode=pl.Buffered(1))


def _params(n_axes):
    return pltpu.CompilerParams(
        dimension_semantics=("arbitrary",) * n_axes, vmem_limit_bytes=VMEM_LIMIT)


def _ffn_body(*refs, d_ff, with_proj, with_ple):
    refs = list(refs)
    h = refs.pop(0)[...]
    if with_proj:
        o_in_ref, w_o_ref, mix_g_ref = refs[:3]
        del refs[:3]
        h = h + _rms(_dot(o_in_ref[...], w_o_ref[...]), mix_g_ref[...])
    pre_g_ref, w_in_ref, w_down_ref, post_g_ref = refs[:4]
    del refs[:4]
    if with_ple:
        p_ref, ple_pre_g_ref, wg_ref, wp_ref, ple_post_g_ref = refs[:5]
        del refs[:5]
    o_ref, act_ref = refs

    xn = _rms(h, pre_g_ref[...]).astype(BF16)
    for c in range(d_ff // FFN_CHUNK):
        lo = c * FFN_CHUNK
        g = _dot(xn, w_in_ref[:, lo:lo + FFN_CHUNK])
        u = _dot(xn, w_in_ref[:, d_ff + lo:d_ff + lo + FFN_CHUNK])
        act_ref[:, lo:lo + FFN_CHUNK] = (g * jax.nn.sigmoid(g) * u).astype(BF16)
    f = _dot(act_ref[...], w_down_ref[...])
    h = h + 0.5 * _rms(f, post_g_ref[...])
    if with_ple:
        xn = _rms(h, ple_pre_g_ref[...]).astype(BF16)
        gate = jax.nn.sigmoid(_dot(xn, wg_ref[...]))
        proj = _dot(p_ref[...].astype(BF16), wp_ref[...])
        h = h + _rms(gate * proj, ple_post_g_ref[...])
    o_ref[...] = h


def _ffn(h, layer, pre_g, w_in, w_down, post_g, proj=None, ple=None):
    t, d = h.shape
    d_ff = w_down.shape[1]
    tm = min(TM_FFN, t)
    row = pl.BlockSpec((tm, d), lambda i: (i, 0))
    args, specs = [h], [row]
    if proj is not None:
        o_in, (lw, w_o), (lg, mix_g) = proj
        args += [o_in, w_o, mix_g]
        specs += [pl.BlockSpec((tm, o_in.shape[1]), lambda i: (i, 0)), _layer_spec(w_o, lw),
                  _layer_spec(mix_g, lg)]
    stacked = [pre_g, w_in, w_down, post_g]
    args += stacked
    specs += [_layer_spec(a, layer) for a in stacked]
    if ple is not None:
        args += list(ple)
        specs += [pl.BlockSpec((None, tm, ple[0].shape[2]), lambda i: (layer, i, 0))]
        specs += [_layer_spec(a, layer) for a in ple[1:]]
    return pl.pallas_call(
        functools.partial(_ffn_body, d_ff=d_ff, with_proj=proj is not None, with_ple=ple is not None),
        grid=(t // tm,),
        in_specs=specs,
        out_specs=row,
        out_shape=jax.ShapeDtypeStruct((t, d), F32),
        scratch_shapes=[pltpu.VMEM((tm, d_ff), BF16)],
        compiler_params=_params(1),
        name="ffn",
    )(*args)


def _mla_proj_body(h_ref, pos_ref, inv_ref, pre_g_ref, w_in_ref, qg_ref, kvg_ref,
                   wq_t_ref, wk_ref, wv_t_ref, q_t_ref, k_ref, v_t_ref, *, q_scale):
    tm = h_ref.shape[0]
    ang = inv_ref[...] * pos_ref[...].astype(F32)
    cos32 = jnp.cos(ang)
    sin32 = jnp.sin(ang)
    pad = jnp.zeros((LANES - MLA_ROPE, tm), F32)
    cos_t = jnp.concatenate([cos32, cos32, pad], axis=0)
    sin_t = jnp.concatenate([-sin32, sin32, pad], axis=0)
    cos_n = cos_t.T
    sin_n = sin_t.T
    hn = _rms(h_ref[...], pre_g_ref[...]).astype(BF16)
    c = _dot(hn, w_in_ref[...])
    cq = _rms(c[:, :MLA_Q_LORA], qg_ref[...])
    ckv = _rms(c[:, MLA_Q_LORA:MLA_Q_LORA + MLA_KV_LORA], kvg_ref[...])
    o = MLA_Q_LORA + MLA_KV_LORA
    kr = (c[:, o:o + LANES] * cos_n + c[:, o + LANES:o + 2 * LANES] * sin_n).astype(BF16)
    cq_t = cq.T.astype(BF16)
    ckv_t = ckv.T.astype(BF16)
    kn = _dot(ckv.astype(BF16), wk_ref[...])
    v_t = _dot(wv_t_ref[...], ckv_t)
    for hd in range(MLA_HEADS):
        k_ref[:, hd * MLA_QK_PAD:hd * MLA_QK_PAD + MLA_NOPE] = (
            kn[:, hd * MLA_NOPE:(hd + 1) * MLA_NOPE].astype(BF16))
        k_ref[:, hd * MLA_QK_PAD + MLA_NOPE:(hd + 1) * MLA_QK_PAD] = kr
        v_t_ref[hd] = v_t[hd * MLA_V:(hd + 1) * MLA_V, :].astype(BF16)
        qh = _dot(wq_t_ref[hd], cq_t)
        qr = qh[MLA_NOPE:MLA_NOPE + LANES] * cos_t + qh[MLA_NOPE + LANES:] * sin_t
        q_t_ref[hd, 0:MLA_NOPE, :] = (qh[:MLA_NOPE] * q_scale).astype(BF16)
        q_t_ref[hd, MLA_NOPE:MLA_QK_PAD, :] = (qr * q_scale).astype(BF16)


def _mla_weights(w_in, w_uq, w_ukv):
    n = w_in.shape[0]
    half = MLA_ROPE // 2

    def rope_slabs(w):
        x1, x2 = w[..., :half], w[..., half:]
        z = jnp.zeros(w.shape[:-1] + (LANES - MLA_ROPE,), w.dtype)
        return jnp.concatenate([x1, x2, z, x2, x1, z], axis=-1)

    o = MLA_Q_LORA + MLA_KV_LORA
    w_in_k = jnp.concatenate([w_in[..., :o], rope_slabs(w_in[..., o:])], axis=-1).astype(BF16)
    wq = w_uq.reshape(n, MLA_Q_LORA, MLA_HEADS, MLA_QK)
    wq = jnp.concatenate([wq[..., :MLA_NOPE], rope_slabs(wq[..., MLA_NOPE:])], axis=-1)
    wq_t = jnp.transpose(wq, (0, 2, 3, 1)).astype(BF16)
    wkv = w_ukv.reshape(n, MLA_KV_LORA, MLA_HEADS, MLA_NOPE + MLA_V)
    wk = wkv[..., :MLA_NOPE].reshape(n, MLA_KV_LORA, MLA_HEADS * MLA_NOPE).astype(BF16)
    wv = wkv[..., MLA_NOPE:].reshape(n, MLA_KV_LORA, MLA_HEADS * MLA_V)
    wv_t = jnp.swapaxes(wv, 1, 2).astype(BF16)
    return w_in_k, wq_t, wk, wv_t


def _mla_proj(h, pos, layer, j, pre_g, weights, q_norm_g, kv_norm_g, qc):
    t, d = h.shape
    half = MLA_ROPE // 2
    inv = 1.0 / (ROPE_BASE ** (jnp.arange(0, MLA_ROPE, 2, dtype=F32) / MLA_ROPE))
    q_scale = (MLA_QK ** -0.5) * math.log2(math.e)
    w_in_k, wq_t, wk, wv_t = weights
    n = t // qc
    return pl.pallas_call(
        functools.partial(_mla_proj_body, q_scale=q_scale),
        grid=(n,),
        in_specs=[pl.BlockSpec((qc, d), lambda i: (i, 0)), pl.BlockSpec((1, qc), lambda i: (0, i)),
                  _const_spec((half, 1)), _layer_spec(pre_g, layer), _layer_spec(w_in_k, j),
                  _layer_spec(q_norm_g, j), _layer_spec(kv_norm_g, j),
                  _layer_spec(wq_t, j), _layer_spec(wk, j), _layer_spec(wv_t, j)],
        out_specs=[pl.BlockSpec((None, MLA_HEADS, MLA_QK_PAD, qc), lambda i: (i, 0, 0, 0)),
                   pl.BlockSpec((qc, MLA_HEADS * MLA_QK_PAD), lambda i: (i, 0)),
                   pl.BlockSpec((None, MLA_HEADS, MLA_V, qc), lambda i: (i, 0, 0, 0))],
        out_shape=[jax.ShapeDtypeStruct((n, MLA_HEADS, MLA_QK_PAD, qc), BF16),
                   jax.ShapeDtypeStruct((t, MLA_HEADS * MLA_QK_PAD), BF16),
                   jax.ShapeDtypeStruct((n, MLA_HEADS, MLA_V, qc), BF16)],
        compiler_params=_params(1),
        name="mla_proj",
    )(h, pos.reshape(1, t), inv.reshape(half, 1), pre_g, w_in_k, q_norm_g, kv_norm_g, wq_t, wk, wv_t)


def _qk_t(k, q_t, mask):
    s_t = _dot(k, q_t)
    if mask is not None:
        s_t = jnp.where(mask, s_t, -jnp.inf)
    return s_t


def _softmax_pv_t(s_t, v_t, m_ref, l_ref, acc_ref, r):
    m_prev = m_ref[r]
    m_new = jnp.maximum(m_prev, jnp.max(s_t, axis=0, keepdims=True))
    alpha = jnp.exp2(m_prev - m_new)
    p_t = jnp.exp2(s_t - m_new)
    l_ref[r] = alpha * l_ref[r] + jnp.sum(p_t, axis=0, keepdims=True)
    acc_ref[r] = _dot(v_t, p_t.astype(BF16)) + alpha * acc_ref[r]
    m_ref[r] = m_new


def _attn_body(q_t_ref, k_ref, v_t_ref, o_ref, m_ref, l_ref, acc_ref, *, tq, tk, qc, unroll):
    i = pl.program_id(2)
    n_sub = tq // qc
    kc = tk // qc
    m_ref[...] = jnp.full(m_ref.shape, -jnp.inf, F32)
    l_ref[...] = jnp.zeros(l_ref.shape, F32)
    acc_ref[...] = jnp.zeros(acc_ref.shape, F32)

    def pipelined(items):
        s_t = items[0][0]()
        for n, (_, v_t, r) in enumerate(items):
            s_next = items[n + 1][0]() if n + 1 < len(items) else None
            _softmax_pv_t(s_t, v_t, m_ref, l_ref, acc_ref, r)
            s_t = s_next

    def body(j, carry):
        items = []
        for u in range(unroll):
            jb = j * unroll + u
            k = k_ref[pl.ds(pl.multiple_of(jb * tk, tk), tk), :]
            v_t = jnp.concatenate([v_t_ref[jb * kc + c] for c in range(kc)], axis=1)
            items += [(functools.partial(_qk_t, k, q_t_ref[r], None), v_t, r) for r in range(n_sub)]
        pipelined(items)
        return carry

    lax.fori_loop(0, i * (tq // tk) // unroll, body, 0)

    d0 = pl.multiple_of(i * tq, tq)
    krow = lax.broadcasted_iota(jnp.int32, (qc, qc), 0)
    qcol = lax.broadcasted_iota(jnp.int32, (qc, qc), 1)

    def diag_scores(r):
        mask = krow <= qcol
        if r:
            mask = jnp.concatenate([jnp.ones((r * qc, qc), jnp.bool_), mask], axis=0)
        return _qk_t(k_ref[pl.ds(d0, (r + 1) * qc), :], q_t_ref[r], mask)

    pipelined([(functools.partial(diag_scores, r),
                jnp.concatenate([v_t_ref[i * n_sub + c] for c in range(r + 1)], axis=1), r)
               for r in range(n_sub)])

    for r in range(n_sub):
        o_t = acc_ref[r] / l_ref[r]
        o_ref[r * qc:(r + 1) * qc, :] = o_t.T.astype(o_ref.dtype)


def _attn(q_t, k, v_t, b, s, qc):
    tq = min(TQ_ATTN, s)
    tk = min(TK_ATTN, tq)
    n_sub = tq // qc
    unroll = math.gcd(KB_UNROLL, tq // tk)
    return pl.pallas_call(
        functools.partial(_attn_body, tq=tq, tk=tk, qc=qc, unroll=unroll),
        grid=(b, MLA_HEADS, s // tq),
        in_specs=[pl.BlockSpec((n_sub, None, MLA_QK_PAD, qc),
                               lambda bi, h, i: (bi * (s // tq) + i, h, 0, 0)),
                  pl.BlockSpec((None, s, MLA_QK_PAD), lambda bi, h, i: (bi, 0, h)),
                  pl.BlockSpec((s // qc, None, MLA_V, qc), lambda bi, h, i: (bi, h, 0, 0))],
        out_specs=pl.BlockSpec((None, tq, MLA_V), lambda bi, h, i: (bi, i, h)),
        out_shape=jax.ShapeDtypeStruct((b, s, MLA_HEADS * MLA_V), BF16),
        scratch_shapes=[pltpu.VMEM((n_sub, 1, qc), F32), pltpu.VMEM((n_sub, 1, qc), F32),
                        pltpu.VMEM((n_sub, MLA_V, qc), F32)],
        compiler_params=_params(3),
        name="attn",
    )(q_t, k.reshape(b, s, MLA_HEADS * MLA_QK_PAD), v_t)


def _softplus(x):
    return jnp.maximum(x, 0.0) + jnp.log1p(jnp.exp(-jnp.abs(x)))


def _hyb_body(h_ref, pre_g_ref, w_in_ref, ln_g_ref, ln_b_ref, ws_ref, bs_ref,
              conv_w_ref, conv_b_ref, dt_bias_ref, a_log_ref, dskip_ref, ng_ref,
              w_out_ref, post_g_ref, o_ref,
              ext_ref, state_ref, y_ref, xs_ref, b_ref, c_ref, z_ref, dt_ref, da_ref, *, tm):
    nc = tm // CHUNK
    t_idx = pl.program_id(1)

    @pl.when(t_idx == 0)
    def _():
        ext_ref[0:CONV_HALO, :] = jnp.zeros((CONV_HALO, SSD_CONV_CH), F32)
        state_ref[...] = jnp.zeros(state_ref.shape, F32)

    h = h_ref[...]
    hn = _rms(h, pre_g_ref[...]).astype(BF16)
    row = lax.broadcasted_iota(jnp.int32, (CHUNK, CHUNK), 0)
    col = lax.broadcasted_iota(jnp.int32, (CHUNK, CHUNK), 1)
    causal = col <= row

    u = jax.nn.gelu(_dot(hn, w_in_ref[:, 0:GM_WIDTH]))
    v = jax.nn.gelu(_dot(hn, w_in_ref[:, GM_WIDTH:2 * GM_WIDTH]))
    for hd in range(GM_HEADS):
        sl = slice(hd * GM_HEAD_DIM, (hd + 1) * GM_HEAD_DIM)
        vh = v[:, sl]
        mu = jnp.mean(vh, axis=-1, keepdims=True)
        xc = vh - mu
        var = jnp.mean(xc * xc, axis=-1, keepdims=True)
        vn = (xc * lax.rsqrt(var + LN_EPS) * ln_g_ref[:, sl] + ln_b_ref[:, sl]).astype(BF16)
        rhs = jnp.concatenate([vn[c * CHUNK:(c + 1) * CHUNK] for c in range(nc)], axis=1)
        w = jnp.where(causal, ws_ref[hd], 0.0).astype(BF16)
        mixed = _dot(w, rhs)
        bias = bs_ref[:, hd:hd + 1]
        for c in range(nc):
            y_ref[c * CHUNK:(c + 1) * CHUNK, sl] = (
                u[c * CHUNK:(c + 1) * CHUNK, sl] * (mixed[:, c * CHUNK:(c + 1) * CHUNK] + bias)
            ).astype(BF16)

    o = 2 * GM_WIDTH
    z_ref[...] = _dot(hn, w_in_ref[:, o:o + SSD_INNER])
    o += SSD_INNER
    ext_ref[CONV_HALO:CONV_HALO + tm, :] = _dot(hn, w_in_ref[:, o:o + SSD_CONV_CH])
    o += SSD_CONV_CH
    dt = _softplus(_dot(hn, w_in_ref[:, o:o + LANES]) + dt_bias_ref[...])
    dt_ref[...] = dt
    da_ref[...] = dt * (-jnp.exp(a_log_ref[...]))

    conv = conv_b_ref[...]
    for kk in range(SSD_CONV):
        lo = CONV_HALO - (SSD_CONV - 1) + kk
        conv = conv + conv_w_ref[kk:kk + 1, :] * ext_ref[lo:lo + tm, :]
    ext_ref[0:CONV_HALO, :] = ext_ref[tm:tm + CONV_HALO, :]
    xbc = conv * jax.nn.sigmoid(conv)
    xs_ref[...] = xbc[:, :SSD_INNER]
    b_ref[...] = xbc[:, SSD_INNER:SSD_INNER + SSD_GROUPS * SSD_STATE]
    c_ref[...] = xbc[:, SSD_INNER + SSD_GROUPS * SSD_STATE:]

    tril = jnp.where(causal, 1.0, 0.0).astype(F32)
    lane = lax.broadcasted_iota(jnp.int32, (CHUNK, LANES), 1)
    first_head = lane < SSD_HEAD_DIM

    def chunk_body(c, carry):
        r0 = pl.multiple_of(c * CHUNK, CHUNK)
        rows = pl.ds(r0, CHUNK)
        dt_c = dt_ref[rows, :]
        acs = jnp.dot(tril, da_ref[rows, :], precision=lax.Precision.HIGHEST,
                      preferred_element_type=F32)
        acs_t = acs.T
        bc = b_ref[rows, :]
        cc = c_ref[rows, :]
        for g in range(SSD_GROUPS):
            gs = slice(g * SSD_STATE, (g + 1) * SSD_STATE)
            bg_t = bc[:, gs].T.astype(BF16)
            cg = cc[:, gs].astype(BF16)
            cb = _dot(cg, bg_t)
            ys = []
            for pj in range(SSD_PAIRS // SSD_GROUPS):
                j = g * (SSD_PAIRS // SSD_GROUPS) + pj
                ha, hb = 2 * j, 2 * j + 1
                sl = slice(j * LANES, (j + 1) * LANES)
                acol_a = jnp.broadcast_to(acs[:, ha:ha + 1], (CHUNK, LANES))
                acol_b = jnp.broadcast_to(acs[:, hb:hb + 1], (CHUNK, LANES))
                arow_a = jnp.broadcast_to(acs_t[ha:ha + 1, :], (CHUNK, CHUNK))
                arow_b = jnp.broadcast_to(acs_t[hb:hb + 1, :], (CHUNK, CHUNK))
                dtw = jnp.where(first_head, jnp.broadcast_to(dt_c[:, ha:ha + 1], (CHUNK, LANES)),
                                jnp.broadcast_to(dt_c[:, hb:hb + 1], (CHUNK, LANES)))
                acw = jnp.where(first_head, acol_a, acol_b)
                xs_p = xs_ref[rows, sl]
                xd = xs_p * dtw
                xd_bf = xd.astype(BF16)
                m_a = (cb * jnp.exp(jnp.where(causal, acol_a - arow_a, -jnp.inf))).astype(BF16)
                m_b = (cb * jnp.exp(jnp.where(causal, acol_b - arow_b, -jnp.inf))).astype(BF16)
                y_diag = jnp.where(first_head, _dot(m_a, xd_bf), _dot(m_b, xd_bf))
                acw_last = acw[CHUNK - 1:CHUNK, :]
                xw = (xd * jnp.exp(acw_last - acw)).astype(BF16)
                st_new = _dot(bg_t, xw)
                prev = state_ref[:, sl]
                y_off = _dot(cg, prev.astype(BF16)) * jnp.exp(acw)
                state_ref[:, sl] = prev * jnp.exp(acw_last) + st_new
                zz = z_ref[rows, sl]
                y = y_diag + y_off + dskip_ref[:, sl] * xs_p
                ys.append(y * (zz * jax.nn.sigmoid(zz)))
            yg = jnp.concatenate(ys, axis=1)
            gw = SSD_INNER // SSD_GROUPS
            yn = _rms(yg, ng_ref[:, g * gw:(g + 1) * gw])
            y_ref[rows, GM_WIDTH + g * gw:GM_WIDTH + (g + 1) * gw] = yn.astype(BF16)
        return carry

    lax.fori_loop(0, nc, chunk_body, 0)

    mixed = _dot(y_ref[...], w_out_ref[...])
    o_ref[...] = h + _rms(mixed, post_g_ref[...])


def _hyb_weights(w_in, ln_g, ln_b, w_s, b_s, conv_w, conv_b, dt_bias, a_log, d_skip, norm_g, w_out):
    n = w_in.shape[0]
    pad = (-w_in.shape[2]) % LANES
    w_in_k = jnp.pad(w_in, ((0, 0), (0, 0), (0, pad))).astype(BF16)
    row = lambda a: a.reshape(n, 1, -1)
    pad_h = lambda a: jnp.pad(row(a), ((0, 0), (0, 0), (0, LANES - SSD_HEADS)))
    return (w_in_k, row(ln_g), row(ln_b), w_s, jnp.swapaxes(b_s, 1, 2), conv_w, row(conv_b),
            pad_h(dt_bias), pad_h(a_log), row(jnp.repeat(d_skip, SSD_HEAD_DIM, axis=1)), row(norm_g),
            w_out.astype(BF16))


def _hybrid(h, layer, j, pre_g, weights, post_g):
    b, s, d = h.shape
    tm = min(TM_HYB, s)
    row = pl.BlockSpec((None, tm, d), lambda bi, i: (bi, i, 0))
    return pl.pallas_call(
        functools.partial(_hyb_body, tm=tm),
        grid=(b, s // tm),
        in_specs=([row, _layer_spec(pre_g, layer)] + [_layer_spec(w, j) for w in weights]
                  + [_layer_spec(post_g, layer)]),
        out_specs=row,
        out_shape=jax.ShapeDtypeStruct((b, s, d), F32),
        scratch_shapes=[
            pltpu.VMEM((tm + CONV_HALO, SSD_CONV_CH), F32),
            pltpu.VMEM((SSD_STATE, SSD_INNER), F32),
            pltpu.VMEM((tm, GM_WIDTH + SSD_INNER), BF16),
            pltpu.VMEM((tm, SSD_INNER), F32),
            pltpu.VMEM((tm, SSD_GROUPS * SSD_STATE), F32),
            pltpu.VMEM((tm, SSD_GROUPS * SSD_STATE), F32),
            pltpu.VMEM((tm, SSD_INNER), F32),
            pltpu.VMEM((tm, LANES), F32),
            pltpu.VMEM((tm, LANES), F32),
        ],
        compiler_params=_params(2),
        name="hybrid",
    )(h, pre_g, *weights, post_g)


def kernel(x, p, positions, ffn1_pre_g, ffn1_w_in, ffn1_w_down, ffn1_post_g, mix_pre_g, mix_post_g, ffn2_pre_g, ffn2_w_in, ffn2_w_down, ffn2_post_g, ple_pre_g, ple_w_gate, ple_w_proj, ple_post_g, hyb_w_in, gm_ln_g, gm_ln_b, gm_w_s, gm_b_s, ssd_conv_w, ssd_conv_b, ssd_dt_bias, ssd_a_log, ssd_d, ssd_norm_g, hyb_w_out, mla_w_in, mla_q_norm_g, mla_kv_norm_g, mla_w_uq, mla_w_ukv, mla_w_out):
    b, s, d = x.shape
    t = b * s
    depth = ffn1_w_in.shape[0]
    g3 = lambda a: a.reshape(a.shape[0], 1, -1)
    bf = lambda a: a.astype(BF16)
    ffn1 = (g3(ffn1_pre_g), bf(ffn1_w_in), bf(ffn1_w_down), g3(ffn1_post_g))
    ffn2 = (g3(ffn2_pre_g), bf(ffn2_w_in), bf(ffn2_w_down), g3(ffn2_post_g))
    ple = (p.reshape(depth, t, -1), g3(ple_pre_g), bf(ple_w_gate), bf(ple_w_proj), g3(ple_post_g))
    mix_pre, mix_post = g3(mix_pre_g), g3(mix_post_g)
    hyb_w = _hyb_weights(hyb_w_in, gm_ln_g, gm_ln_b, gm_w_s, gm_b_s, ssd_conv_w, ssd_conv_b,
                         ssd_dt_bias, ssd_a_log, ssd_d, ssd_norm_g, hyb_w_out)
    mla_w = _mla_weights(mla_w_in, mla_w_uq, mla_w_ukv)
    mla_w_o = bf(mla_w_out)
    qc = min(QC_ATTN, s)
    h = x.reshape(t, d)
    for i in range(depth):
        j = i // 2
        h = _ffn(h, i, *ffn1)
        if i % 2 == 0:
            h = _hybrid(h.reshape(b, s, d), i, j, mix_pre, hyb_w, mix_post).reshape(t, d)
            h = _ffn(h, i, *ffn2, ple=ple)
        else:
            q_t, k, v_t = _mla_proj(h, positions, i, j, mix_pre, mla_w, g3(mla_q_norm_g),
                                    g3(mla_kv_norm_g), qc)
            o = _attn(q_t, k, v_t, b, s, qc)
            h = _ffn(h, i, *ffn2, proj=(o.reshape(t, -1), (j, mla_w_o), (i, mix_post)), ple=ple)
    return h.reshape(b, s, d)
```

```python
import functools
import math

import jax
import jax.numpy as jnp
from jax import lax
from jax.experimental import pallas as pl
from jax.experimental.pallas import tpu as pltpu

F32 = jnp.float32
BF16 = jnp.bfloat16

NORM_EPS = 1e-6
LN_EPS = 1e-5

LANES = 128
CHUNK = 128

GM_HEADS = 8
GM_HEAD_DIM = 128
GM_WIDTH = GM_HEADS * GM_HEAD_DIM

SSD_HEAD_DIM = 64
SSD_HEADS = 16
SSD_INNER = SSD_HEADS * SSD_HEAD_DIM
SSD_GROUPS = 2
SSD_STATE = 128
SSD_CONV = 4
SSD_CONV_CH = SSD_INNER + 2 * SSD_GROUPS * SSD_STATE
SSD_PAIRS = SSD_HEADS // 2
CONV_HALO = 8

MLA_HEADS = 16
MLA_NOPE = 128
MLA_ROPE = 64
MLA_V = 128
MLA_Q_LORA = 256
MLA_KV_LORA = 128
MLA_QK = MLA_NOPE + MLA_ROPE
MLA_QK_PAD = 256
ROPE_BASE = 10000.0
OFFSET_ROW = MLA_QK
FAST_LIMIT = 60.0

FFN_CHUNK = 256
VMEM_LIMIT = 56 * 1024 * 1024

TM_FFN = 512
TM_HYB = 512
QC_ATTN = 512
TQ_ATTN = 2048
TK_ATTN = 1024
KB_UNROLL = 2


def _rms(x, g):
    return x * lax.rsqrt(jnp.mean(x * x, axis=-1, keepdims=True) + NORM_EPS) * g


def _dot(a, b):
    return jnp.dot(a, b, preferred_element_type=F32)


def _const_spec(shape):
    nd = len(shape)
    return pl.BlockSpec(shape, lambda *_: (0,) * nd, pipeline_mode=pl.Buffered(1))


def _layer_spec(stacked, layer):
    zeros = (0,) * (stacked.ndim - 1)
    return pl.BlockSpec((None,) + stacked.shape[1:], lambda *_: (layer,) + zeros,
                        pipeline_mode=pl.Buffered(1))


def _params(n_axes):
    return pltpu.CompilerParams(
        dimension_semantics=("arbitrary",) * n_axes, vmem_limit_bytes=VMEM_LIMIT)


def _ffn_body(*refs, d_ff, with_proj, with_ple):
    refs = list(refs)
    h = refs.pop(0)[...]
    if with_proj:
        o_in_ref, w_o_ref, mix_g_ref = refs[:3]
        del refs[:3]
        h = h + _rms(_dot(o_in_ref[...], w_o_ref[...]), mix_g_ref[...])
    pre_g_ref, w_in_ref, w_down_ref, post_g_ref = refs[:4]
    del refs[:4]
    if with_ple:
        p_ref, ple_pre_g_ref, wg_ref, wp_ref, ple_post_g_ref = refs[:5]
        del refs[:5]
    o_ref, act_ref = refs

    xn = _rms(h, pre_g_ref[...]).astype(BF16)
    for c in range(d_ff // FFN_CHUNK):
        lo = c * FFN_CHUNK
        g = _dot(xn, w_in_ref[:, lo:lo + FFN_CHUNK])
        u = _dot(xn, w_in_ref[:, d_ff + lo:d_ff + lo + FFN_CHUNK])
        act_ref[:, lo:lo + FFN_CHUNK] = (g * jax.nn.sigmoid(g) * u).astype(BF16)
    f = _dot(act_ref[...], w_down_ref[...])
    h = h + 0.5 * _rms(f, post_g_ref[...])
    if with_ple:
        xn = _rms(h, ple_pre_g_ref[...]).astype(BF16)
        gate = jax.nn.sigmoid(_dot(xn, wg_ref[...]))
        proj = _dot(p_ref[...].astype(BF16), wp_ref[...])
        h = h + _rms(gate * proj, ple_post_g_ref[...])
    o_ref[...] = h


def _ffn(h, layer, pre_g, w_in, w_down, post_g, proj=None, ple=None):
    t, d = h.shape
    d_ff = w_down.shape[1]
    tm = min(TM_FFN, t)
    row = pl.BlockSpec((tm, d), lambda i: (i, 0))
    args, specs = [h], [row]
    if proj is not None:
        o_in, (lw, w_o), (lg, mix_g) = proj
        args += [o_in, w_o, mix_g]
        specs += [pl.BlockSpec((tm, o_in.shape[1]), lambda i: (i, 0)), _layer_spec(w_o, lw),
                  _layer_spec(mix_g, lg)]
    stacked = [pre_g, w_in, w_down, post_g]
    args += stacked
    specs += [_layer_spec(a, layer) for a in stacked]
    if ple is not None:
        args += list(ple)
        specs += [pl.BlockSpec((None, tm, ple[0].shape[2]), lambda i: (layer, i, 0))]
        specs += [_layer_spec(a, layer) for a in ple[1:]]
    return pl.pallas_call(
        functools.partial(_ffn_body, d_ff=d_ff, with_proj=proj is not None, with_ple=ple is not None),
        grid=(t // tm,),
        in_specs=specs,
        out_specs=row,
        out_shape=jax.ShapeDtypeStruct((t, d), F32),
        scratch_shapes=[pltpu.VMEM((tm, d_ff), BF16)],
        compiler_params=_params(1),
        name="ffn",
    )(*args)


def _mla_proj_body(h_ref, pos_ref, inv_ref, pre_g_ref, w_in_ref, qg_ref, kvg_ref,
                   wq_t_ref, wk_ref, wv_t_ref, q_t_ref, k_ref, v_t_ref, *, q_scale):
    tm = h_ref.shape[0]
    ang = inv_ref[...] * pos_ref[...].astype(F32)
    cos32 = jnp.cos(ang)
    sin32 = jnp.sin(ang)
    pad = jnp.zeros((LANES - MLA_ROPE, tm), F32)
    cos_t = jnp.concatenate([cos32, cos32, pad], axis=0)
    sin_t = jnp.concatenate([-sin32, sin32, pad], axis=0)
    cos_n = cos_t.T
    sin_n = sin_t.T
    hn = _rms(h_ref[...], pre_g_ref[...]).astype(BF16)
    c = _dot(hn, w_in_ref[...])
    cq = _rms(c[:, :MLA_Q_LORA], qg_ref[...])
    ckv = _rms(c[:, MLA_Q_LORA:MLA_Q_LORA + MLA_KV_LORA], kvg_ref[...])
    o = MLA_Q_LORA + MLA_KV_LORA
    one = jnp.where(lax.broadcasted_iota(jnp.int32, (1, LANES), 1) == OFFSET_ROW - MLA_NOPE, 1.0, 0.0)
    kr = (c[:, o:o + LANES] * cos_n + c[:, o + LANES:o + 2 * LANES] * sin_n + one).astype(BF16)
    cq_t = cq.T.astype(BF16)
    ckv_t = ckv.T.astype(BF16)
    kn = _dot(ckv.astype(BF16), wk_ref[...])
    v_t = _dot(wv_t_ref[...], ckv_t)
    for hd in range(MLA_HEADS):
        k_ref[:, hd * MLA_QK_PAD:hd * MLA_QK_PAD + MLA_NOPE] = (
            kn[:, hd * MLA_NOPE:(hd + 1) * MLA_NOPE].astype(BF16))
        k_ref[:, hd * MLA_QK_PAD + MLA_NOPE:(hd + 1) * MLA_QK_PAD] = kr
        v_t_ref[hd] = v_t[hd * MLA_V:(hd + 1) * MLA_V, :].astype(BF16)
        qh = _dot(wq_t_ref[hd], cq_t)
        qr = qh[MLA_NOPE:MLA_NOPE + LANES] * cos_t + qh[MLA_NOPE + LANES:] * sin_t
        q_t_ref[hd, 0:MLA_NOPE, :] = (qh[:MLA_NOPE] * q_scale).astype(BF16)
        q_t_ref[hd, MLA_NOPE:MLA_QK_PAD, :] = (qr * q_scale).astype(BF16)


def _mla_weights(w_in, w_uq, w_ukv):
    n = w_in.shape[0]
    half = MLA_ROPE // 2

    def rope_slabs(w):
        x1, x2 = w[..., :half], w[..., half:]
        z = jnp.zeros(w.shape[:-1] + (LANES - MLA_ROPE,), w.dtype)
        return jnp.concatenate([x1, x2, z, x2, x1, z], axis=-1)

    o = MLA_Q_LORA + MLA_KV_LORA
    w_in_k = jnp.concatenate([w_in[..., :o], rope_slabs(w_in[..., o:])], axis=-1).astype(BF16)
    wq = w_uq.reshape(n, MLA_Q_LORA, MLA_HEADS, MLA_QK)
    wq = jnp.concatenate([wq[..., :MLA_NOPE], rope_slabs(wq[..., MLA_NOPE:])], axis=-1)
    wq_t = jnp.transpose(wq, (0, 2, 3, 1)).astype(BF16)
    wkv = w_ukv.reshape(n, MLA_KV_LORA, MLA_HEADS, MLA_NOPE + MLA_V)
    wk = wkv[..., :MLA_NOPE].reshape(n, MLA_KV_LORA, MLA_HEADS * MLA_NOPE).astype(BF16)
    wv = wkv[..., MLA_NOPE:].reshape(n, MLA_KV_LORA, MLA_HEADS * MLA_V)
    wv_t = jnp.swapaxes(wv, 1, 2).astype(BF16)
    return w_in_k, wq_t, wk, wv_t


def _mla_proj(h, pos, layer, j, pre_g, weights, q_norm_g, kv_norm_g, qc):
    t, d = h.shape
    half = MLA_ROPE // 2
    inv = 1.0 / (ROPE_BASE ** (jnp.arange(0, MLA_ROPE, 2, dtype=F32) / MLA_ROPE))
    q_scale = (MLA_QK ** -0.5) * math.log2(math.e)
    w_in_k, wq_t, wk, wv_t = weights
    n = t // qc
    return pl.pallas_call(
        functools.partial(_mla_proj_body, q_scale=q_scale),
        grid=(n,),
        in_specs=[pl.BlockSpec((qc, d), lambda i: (i, 0)), pl.BlockSpec((1, qc), lambda i: (0, i)),
                  _const_spec((half, 1)), _layer_spec(pre_g, layer), _layer_spec(w_in_k, j),
                  _layer_spec(q_norm_g, j), _layer_spec(kv_norm_g, j),
                  _layer_spec(wq_t, j), _layer_spec(wk, j), _layer_spec(wv_t, j)],
        out_specs=[pl.BlockSpec((None, MLA_HEADS, MLA_QK_PAD, qc), lambda i: (i, 0, 0, 0)),
                   pl.BlockSpec((qc, MLA_HEADS * MLA_QK_PAD), lambda i: (i, 0)),
                   pl.BlockSpec((None, MLA_HEADS, MLA_V, qc), lambda i: (i, 0, 0, 0))],
        out_shape=[jax.ShapeDtypeStruct((n, MLA_HEADS, MLA_QK_PAD, qc), BF16),
                   jax.ShapeDtypeStruct((t, MLA_HEADS * MLA_QK_PAD), BF16),
                   jax.ShapeDtypeStruct((n, MLA_HEADS, MLA_V, qc), BF16)],
        compiler_params=_params(1),
        name="mla_proj",
    )(h, pos.reshape(1, t), inv.reshape(half, 1), pre_g, w_in_k, q_norm_g, kv_norm_g, wq_t, wk, wv_t)


def _qk_t(k, q_t, mask):
    s_t = _dot(k, q_t)
    if mask is not None:
        s_t = jnp.where(mask, s_t, -jnp.inf)
    return s_t


def _softmax_pv_t(s_t, v_t, m_ref, l_ref, acc_ref, r):
    m_prev = m_ref[r]
    m_new = jnp.maximum(m_prev, jnp.max(s_t, axis=0, keepdims=True))
    alpha = jnp.exp2(m_prev - m_new)
    p_t = jnp.exp2(s_t - m_new)
    l_ref[r] = alpha * l_ref[r] + jnp.sum(p_t, axis=0, keepdims=True)
    acc_ref[r] = _dot(v_t, p_t.astype(BF16)) + alpha * acc_ref[r]
    m_ref[r] = m_new


def _lagged_pv_t(s_t, v_t, m_ref, l_ref, acc_ref, qa_ref, excess_ref, r):
    p_t = jnp.exp2(s_t)
    top = jnp.max(s_t, axis=0, keepdims=True)
    l_new = l_ref[r] + jnp.sum(p_t, axis=0, keepdims=True)
    acc_new = _dot(v_t, p_t.astype(BF16)) + acc_ref[r]
    m_old = m_ref[r]
    m_new = (m_old + jnp.maximum(top, 0.0)).astype(BF16).astype(F32)
    alpha = jnp.exp2(m_old - m_new)
    l_ref[r] = l_new * alpha
    acc_ref[r] = acc_new * alpha
    m_ref[r] = m_new
    qa_ref[r, OFFSET_ROW:OFFSET_ROW + 1, :] = (-m_new).astype(BF16)
    excess_ref[...] = jnp.maximum(excess_ref[...], top)


def _attn_body(q_t_ref, k_ref, v_t_ref, o_ref, m_ref, l_ref, acc_ref, qa_ref, excess_ref,
               *, tq, tk, qc, unroll):
    i = pl.program_id(2)
    n_sub = tq // qc
    kc = tk // qc
    n_blocks = i * (tq // tk) // unroll
    d0 = pl.multiple_of(i * tq, tq)
    krow = lax.broadcasted_iota(jnp.int32, (qc, qc), 0)
    qcol = lax.broadcasted_iota(jnp.int32, (qc, qc), 1)
    tri = krow <= qcol

    def reset():
        m_ref[...] = jnp.full(m_ref.shape, -jnp.inf, F32)
        l_ref[...] = jnp.zeros(l_ref.shape, F32)
        acc_ref[...] = jnp.zeros(acc_ref.shape, F32)

    def exact(s_t, v_t, r):
        _softmax_pv_t(s_t, v_t, m_ref, l_ref, acc_ref, r)

    def lagged(s_t, v_t, r):
        _lagged_pv_t(s_t, v_t, m_ref, l_ref, acc_ref, qa_ref, excess_ref, r)

    def pipelined(items, update):
        s_t = items[0][0]()
        for n, (_, v_t, r) in enumerate(items):
            more = n + 1 < len(items)
            ahead = more and items[n + 1][2] != r
            s_next = items[n + 1][0]() if ahead else None
            update(s_t, v_t, r)
            s_t = items[n + 1][0]() if more and not ahead else s_next

    def tile_keys(c0, c1):
        k = k_ref[pl.ds(pl.multiple_of(d0 + c0 * qc, qc), (c1 - c0) * qc), :]
        v_t = jnp.concatenate([v_t_ref[i * n_sub + c] for c in range(c0, c1)], axis=1)
        return k, v_t

    def block_keys(jb):
        k = k_ref[pl.ds(pl.multiple_of(jb * tk, tk), tk), :]
        v_t = jnp.concatenate([v_t_ref[jb * kc + c] for c in range(kc)], axis=1)
        return k, v_t

    def block_items(j, q_ref):
        items = []
        for u in range(unroll):
            k, v_t = block_keys(j * unroll + u)
            items += [(lambda k=k, r=r: _qk_t(k, q_ref[r], None), v_t, r) for r in range(n_sub)]
        return items

    def finish():
        for r in range(n_sub):
            o_t = acc_ref[r] / l_ref[r]
            o_ref[r * qc:(r + 1) * qc, :] = o_t.T.astype(o_ref.dtype)

    reset()
    own = []
    for r in range(n_sub):
        k, v_t = tile_keys(r, r + 1)
        own.append((lambda k=k, r=r: _qk_t(k, q_t_ref[r], tri), v_t, r))
    pipelined(own, exact)

    for r in range(n_sub):
        m = m_ref[r]
        m_b = m.astype(BF16).astype(F32)
        shift = jnp.exp2(m - m_b)
        l_ref[r] = l_ref[r] * shift
        acc_ref[r] = acc_ref[r] * shift
        m_ref[r] = m_b
        qa_ref[r] = q_t_ref[r]
        qa_ref[r, OFFSET_ROW:OFFSET_ROW + 1, :] = (-m_b).astype(BF16)
    excess_ref[...] = jnp.zeros(excess_ref.shape, F32)

    rest = []
    for r in range(1, n_sub):
        for c0 in range(0, r, kc):
            k, v_t = tile_keys(c0, min(c0 + kc, r))
            rest.append((lambda k=k, r=r: _qk_t(k, qa_ref[r], None), v_t, r))
    if rest:
        pipelined(rest, lagged)

    def lagged_body(j, carry):
        pipelined(block_items(j, qa_ref), lagged)
        return carry

    lax.fori_loop(0, n_blocks, lagged_body, 0)
    finish()

    @pl.when(jnp.max(excess_ref[...]) > FAST_LIMIT)
    def _():
        reset()

        def exact_body(j, carry):
            pipelined(block_items(j, q_t_ref), exact)
            return carry

        lax.fori_loop(0, n_blocks, exact_body, 0)
        diag = []
        for r in range(n_sub):
            k, v_t = tile_keys(0, r + 1)
            mask = tri if r == 0 else jnp.concatenate([jnp.ones((r * qc, qc), jnp.bool_), tri], axis=0)
            diag.append((lambda k=k, r=r, mask=mask: _qk_t(k, q_t_ref[r], mask), v_t, r))
        pipelined(diag, exact)
        finish()


def _attn(q_t, k, v_t, b, s, qc):
    tq = min(TQ_ATTN, s)
    tk = min(TK_ATTN, tq)
    n_sub = tq // qc
    unroll = math.gcd(KB_UNROLL, tq // tk)
    return pl.pallas_call(
        functools.partial(_attn_body, tq=tq, tk=tk, qc=qc, unroll=unroll),
        grid=(b, MLA_HEADS, s // tq),
        in_specs=[pl.BlockSpec((n_sub, None, MLA_QK_PAD, qc),
                               lambda bi, h, i: (bi * (s // tq) + i, h, 0, 0)),
                  pl.BlockSpec((None, s, MLA_QK_PAD), lambda bi, h, i: (bi, 0, h)),
                  pl.BlockSpec((s // qc, None, MLA_V, qc), lambda bi, h, i: (bi, h, 0, 0))],
        out_specs=pl.BlockSpec((None, tq, MLA_V), lambda bi, h, i: (bi, i, h)),
        out_shape=jax.ShapeDtypeStruct((b, s, MLA_HEADS * MLA_V), BF16),
        scratch_shapes=[pltpu.VMEM((n_sub, 1, qc), F32), pltpu.VMEM((n_sub, 1, qc), F32),
                        pltpu.VMEM((n_sub, MLA_V, qc), F32),
                        pltpu.VMEM((n_sub, MLA_QK_PAD, qc), BF16),
                        pltpu.VMEM((1, qc), F32)],
        compiler_params=_params(3),
        name="attn",
    )(q_t, k.reshape(b, s, MLA_HEADS * MLA_QK_PAD), v_t)


def _softplus(x):
    return jnp.maximum(x, 0.0) + jnp.log1p(jnp.exp(-jnp.abs(x)))


def _hyb_body(h_ref, pre_g_ref, w_in_ref, ln_g_ref, ln_b_ref, ws_ref, bs_ref,
              conv_w_ref, conv_b_ref, dt_bias_ref, a_log_ref, dskip_ref, ng_ref,
              w_out_ref, post_g_ref, o_ref,
              ext_ref, state_ref, y_ref, xs_ref, b_ref, c_ref, z_ref, dt_ref, da_ref, *, tm):
    nc = tm // CHUNK
    t_idx = pl.program_id(1)

    @pl.when(t_idx == 0)
    def _():
        ext_ref[0:CONV_HALO, :] = jnp.zeros((CONV_HALO, SSD_CONV_CH), F32)
        state_ref[...] = jnp.zeros(state_ref.shape, F32)

    h = h_ref[...]
    hn = _rms(h, pre_g_ref[...]).astype(BF16)
    row = lax.broadcasted_iota(jnp.int32, (CHUNK, CHUNK), 0)
    col = lax.broadcasted_iota(jnp.int32, (CHUNK, CHUNK), 1)
    causal = col <= row

    u = jax.nn.gelu(_dot(hn, w_in_ref[:, 0:GM_WIDTH]))
    v = jax.nn.gelu(_dot(hn, w_in_ref[:, GM_WIDTH:2 * GM_WIDTH]))
    for hd in range(GM_HEADS):
        sl = slice(hd * GM_HEAD_DIM, (hd + 1) * GM_HEAD_DIM)
        vh = v[:, sl]
        mu = jnp.mean(vh, axis=-1, keepdims=True)
        xc = vh - mu
        var = jnp.mean(xc * xc, axis=-1, keepdims=True)
        vn = (xc * lax.rsqrt(var + LN_EPS) * ln_g_ref[:, sl] + ln_b_ref[:, sl]).astype(BF16)
        rhs = jnp.concatenate([vn[c * CHUNK:(c + 1) * CHUNK] for c in range(nc)], axis=1)
        w = jnp.where(causal, ws_ref[hd], 0.0).astype(BF16)
        mixed = _dot(w, rhs)
        bias = bs_ref[:, hd:hd + 1]
        for c in range(nc):
            y_ref[c * CHUNK:(c + 1) * CHUNK, sl] = (
                u[c * CHUNK:(c + 1) * CHUNK, sl] * (mixed[:, c * CHUNK:(c + 1) * CHUNK] + bias)
            ).astype(BF16)

    o = 2 * GM_WIDTH
    z_ref[...] = _dot(hn, w_in_ref[:, o:o + SSD_INNER])
    o += SSD_INNER
    ext_ref[CONV_HALO:CONV_HALO + tm, :] = _dot(hn, w_in_ref[:, o:o + SSD_CONV_CH])
    o += SSD_CONV_CH
    dt = _softplus(_dot(hn, w_in_ref[:, o:o + LANES]) + dt_bias_ref[...])
    dt_ref[...] = dt
    da_ref[...] = dt * (-jnp.exp(a_log_ref[...]))

    conv = conv_b_ref[...]
    for kk in range(SSD_CONV):
        lo = CONV_HALO - (SSD_CONV - 1) + kk
        conv = conv + conv_w_ref[kk:kk + 1, :] * ext_ref[lo:lo + tm, :]
    ext_ref[0:CONV_HALO, :] = ext_ref[tm:tm + CONV_HALO, :]
    xbc = conv * jax.nn.sigmoid(conv)
    xs_ref[...] = xbc[:, :SSD_INNER]
    b_ref[...] = xbc[:, SSD_INNER:SSD_INNER + SSD_GROUPS * SSD_STATE]
    c_ref[...] = xbc[:, SSD_INNER + SSD_GROUPS * SSD_STATE:]

    tril = jnp.where(causal, 1.0, 0.0).astype(F32)
    lane = lax.broadcasted_iota(jnp.int32, (CHUNK, LANES), 1)
    first_head = lane < SSD_HEAD_DIM

    def chunk_body(c, carry):
        rows = pl.ds(c * CHUNK, CHUNK)
        dt_c = dt_ref[rows, :]
        acs = jnp.dot(tril, da_ref[rows, :], precision=lax.Precision.HIGHEST,
                      preferred_element_type=F32)
        acs_t = acs.T
        bc = b_ref[rows, :]
        cc = c_ref[rows, :]
        for g in range(SSD_GROUPS):
            gs = slice(g * SSD_STATE, (g + 1) * SSD_STATE)
            bg_t = bc[:, gs].T.astype(BF16)
            cg = cc[:, gs].astype(BF16)
            cb = _dot(cg, bg_t)
            ys = []
            for pj in range(SSD_PAIRS // SSD_GROUPS):
                j = g * (SSD_PAIRS // SSD_GROUPS) + pj
                ha, hb = 2 * j, 2 * j + 1
                sl = slice(j * LANES, (j + 1) * LANES)
                acol_a = jnp.broadcast_to(acs[:, ha:ha + 1], (CHUNK, LANES))
                acol_b = jnp.broadcast_to(acs[:, hb:hb + 1], (CHUNK, LANES))
                arow_a = jnp.broadcast_to(acs_t[ha:ha + 1, :], (CHUNK, CHUNK))
                arow_b = jnp.broadcast_to(acs_t[hb:hb + 1, :], (CHUNK, CHUNK))
                dtw = jnp.where(first_head, jnp.broadcast_to(dt_c[:, ha:ha + 1], (CHUNK, LANES)),
                                jnp.broadcast_to(dt_c[:, hb:hb + 1], (CHUNK, LANES)))
                acw = jnp.where(first_head, acol_a, acol_b)
                xs_p = xs_ref[rows, sl]
                xd = xs_p * dtw
                xd_bf = xd.astype(BF16)
                m_a = (cb * jnp.exp(jnp.where(causal, acol_a - arow_a, -jnp.inf))).astype(BF16)
                m_b = (cb * jnp.exp(jnp.where(causal, acol_b - arow_b, -jnp.inf))).astype(BF16)
                y_diag = jnp.where(first_head, _dot(m_a, xd_bf), _dot(m_b, xd_bf))
                acw_last = acw[CHUNK - 1:CHUNK, :]
                xw = (xd * jnp.exp(acw_last - acw)).astype(BF16)
                st_new = _dot(bg_t, xw)
                prev = state_ref[:, sl]
                y_off = _dot(cg, prev.astype(BF16)) * jnp.exp(acw)
                state_ref[:, sl] = prev * jnp.exp(acw_last) + st_new
                zz = z_ref[rows, sl]
                y = y_diag + y_off + dskip_ref[:, sl] * xs_p
                ys.append(y * (zz * jax.nn.sigmoid(zz)))
            yg = jnp.concatenate(ys, axis=1)
            gw = SSD_INNER // SSD_GROUPS
            yn = _rms(yg, ng_ref[:, g * gw:(g + 1) * gw])
            y_ref[rows, GM_WIDTH + g * gw:GM_WIDTH + (g + 1) * gw] = yn.astype(BF16)
        return carry

    for c in range(nc):
        chunk_body(c, 0)

    mixed = _dot(y_ref[...], w_out_ref[...])
    o_ref[...] = h + _rms(mixed, post_g_ref[...])


def _hyb_weights(w_in, ln_g, ln_b, w_s, b_s, conv_w, conv_b, dt_bias, a_log, d_skip, norm_g, w_out):
    n = w_in.shape[0]
    pad = (-w_in.shape[2]) % LANES
    w_in_k = jnp.pad(w_in, ((0, 0), (0, 0), (0, pad))).astype(BF16)
    row = lambda a: a.reshape(n, 1, -1)
    pad_h = lambda a: jnp.pad(row(a), ((0, 0), (0, 0), (0, LANES - SSD_HEADS)))
    return (w_in_k, row(ln_g), row(ln_b), w_s, jnp.swapaxes(b_s, 1, 2), conv_w, row(conv_b),
            pad_h(dt_bias), pad_h(a_log), row(jnp.repeat(d_skip, SSD_HEAD_DIM, axis=1)), row(norm_g),
            w_out.astype(BF16))


def _hybrid(h, layer, j, pre_g, weights, post_g):
    b, s, d = h.shape
    tm = min(TM_HYB, s)
    row = pl.BlockSpec((None, tm, d), lambda bi, i: (bi, i, 0))
    return pl.pallas_call(
        functools.partial(_hyb_body, tm=tm),
        grid=(b, s // tm),
        in_specs=([row, _layer_spec(pre_g, layer)] + [_layer_spec(w, j) for w in weights]
                  + [_layer_spec(post_g, layer)]),
        out_specs=row,
        out_shape=jax.ShapeDtypeStruct((b, s, d), F32),
        scratch_shapes=[
            pltpu.VMEM((tm + CONV_HALO, SSD_CONV_CH), F32),
            pltpu.VMEM((SSD_STATE, SSD_INNER), F32),
            pltpu.VMEM((tm, GM_WIDTH + SSD_INNER), BF16),
            pltpu.VMEM((tm, SSD_INNER), F32),
            pltpu.VMEM((tm, SSD_GROUPS * SSD_STATE), F32),
            pltpu.VMEM((tm, SSD_GROUPS * SSD_STATE), F32),
            pltpu.VMEM((tm, SSD_INNER), F32),
            pltpu.VMEM((tm, LANES), F32),
            pltpu.VMEM((tm, LANES), F32),
        ],
        compiler_params=_params(2),
        name="hybrid",
    )(h, pre_g, *weights, post_g)


def kernel(x, p, positions, ffn1_pre_g, ffn1_w_in, ffn1_w_down, ffn1_post_g, mix_pre_g, mix_post_g, ffn2_pre_g, ffn2_w_in, ffn2_w_down, ffn2_post_g, ple_pre_g, ple_w_gate, ple_w_proj, ple_post_g, hyb_w_in, gm_ln_g, gm_ln_b, gm_w_s, gm_b_s, ssd_conv_w, ssd_conv_b, ssd_dt_bias, ssd_a_log, ssd_d, ssd_norm_g, hyb_w_out, mla_w_in, mla_q_norm_g, mla_kv_norm_g, mla_w_uq, mla_w_ukv, mla_w_out):
    b, s, d = x.shape
    t = b * s
    depth = ffn1_w_in.shape[0]
    g3 = lambda a: a.reshape(a.shape[0], 1, -1)
    bf = lambda a: a.astype(BF16)
    ffn1 = (g3(ffn1_pre_g), bf(ffn1_w_in), bf(ffn1_w_down), g3(ffn1_post_g))
    ffn2 = (g3(ffn2_pre_g), bf(ffn2_w_in), bf(ffn2_w_down), g3(ffn2_post_g))
    ple = (p.reshape(depth, t, -1), g3(ple_pre_g), bf(ple_w_gate), bf(ple_w_proj), g3(ple_post_g))
    mix_pre, mix_post = g3(mix_pre_g), g3(mix_post_g)
    hyb_w = _hyb_weights(hyb_w_in, gm_ln_g, gm_ln_b, gm_w_s, gm_b_s, ssd_conv_w, ssd_conv_b,
                         ssd_dt_bias, ssd_a_log, ssd_d, ssd_norm_g, hyb_w_out)
    mla_w = _mla_weights(mla_w_in, mla_w_uq, mla_w_ukv)
    mla_w_o = bf(mla_w_out)
    qc = min(QC_ATTN, s)
    h = x.reshape(t, d)
    for i in range(depth):
        j = i // 2
        h = _ffn(h, i, *ffn1)
        if i % 2 == 0:
            h = _hybrid(h.reshape(b, s, d), i, j, mix_pre, hyb_w, mix_post).reshape(t, d)
            h = _ffn(h, i, *ffn2, ple=ple)
        else:
            q_t, k, v_t = _mla_proj(h, positions, i, j, mix_pre, mla_w, g3(mla_q_norm_g),
                                    g3(mla_kv_norm_g), qc)
            o = _attn(q_t, k, v_t, b, s, qc)
            h = _ffn(h, i, *ffn2, proj=(o.reshape(t, -1), (j, mla_w_o), (i, mix_post)), ple=ple)
    return h.reshape(b, s, d)
```

```python
import functools
import math

import jax
import jax.numpy as jnp
from jax import lax
from jax.experimental import pallas as pl
from jax.experimental.pallas import tpu as pltpu

F32 = jnp.float32
BF16 = jnp.bfloat16

NORM_EPS = 1e-6
LN_EPS = 1e-5

LANES = 128
CHUNK = 128

GM_HEADS = 8
GM_HEAD_DIM = 128
GM_WIDTH = GM_HEADS * GM_HEAD_DIM

SSD_HEAD_DIM = 64
SSD_HEADS = 16
SSD_INNER = SSD_HEADS * SSD_HEAD_DIM
SSD_GROUPS = 2
SSD_STATE = 128
SSD_CONV = 4
SSD_CONV_CH = SSD_INNER + 2 * SSD_GROUPS * SSD_STATE
SSD_PAIRS = SSD_HEADS // 2
CONV_HALO = 8

MLA_HEADS = 16
MLA_NOPE = 128
MLA_ROPE = 64
MLA_V = 128
MLA_Q_LORA = 256
MLA_KV_LORA = 128
MLA_QK = MLA_NOPE + MLA_ROPE
MLA_QK_PAD = 256
ROPE_BASE = 10000.0
OFFSET_ROW = MLA_QK
FAST_LIMIT = 60.0

FFN_CHUNK = 256
FFN_GROUP = 256
HYB_SPLIT = 2
VMEM_LIMIT = 56 * 1024 * 1024

TM_FFN = 512
TM_HYB = 512
QC_ATTN = 512
TQ_ATTN = 2048
TK_ATTN = 1024
KB_UNROLL = 2


def _rms(x, g):
    return x * lax.rsqrt(jnp.mean(x * x, axis=-1, keepdims=True) + NORM_EPS) * g


def _dot(a, b):
    return jnp.dot(a, b, preferred_element_type=F32)


def _const_spec(shape):
    nd = len(shape)
    return pl.BlockSpec(shape, lambda *_: (0,) * nd, pipeline_mode=pl.Buffered(1))


def _layer_spec(stacked, layer):
    zeros = (0,) * (stacked.ndim - 1)
    return pl.BlockSpec((None,) + stacked.shape[1:], lambda *_: (layer,) + zeros,
                        pipeline_mode=pl.Buffered(1))


def _params(n_axes):
    return pltpu.CompilerParams(
        dimension_semantics=("arbitrary",) * n_axes, vmem_limit_bytes=VMEM_LIMIT)


def _ffn_body(*refs, d_ff, with_proj, with_ple):
    refs = list(refs)
    h_ref = refs.pop(0)
    tm = h_ref.shape[0]
    grp = min(FFN_GROUP, tm)
    parts = [slice(r0, r0 + grp) for r0 in range(0, tm, grp)]
    h = [h_ref[p, :] for p in parts]
    if with_proj:
        o_in_ref, w_o_ref, mix_g_ref = refs[:3]
        del refs[:3]
        mixed = [_dot(o_in_ref[p, :], w_o_ref[...]) for p in parts]
        h = [hh + _rms(mm, mix_g_ref[...]) for hh, mm in zip(h, mixed)]
    pre_g_ref, w_in_ref, w_down_ref, post_g_ref = refs[:4]
    del refs[:4]
    if with_ple:
        p_ref, ple_pre_g_ref, wg_ref, wp_ref, ple_post_g_ref = refs[:5]
        del refs[:5]
    o_ref, act_ref = refs

    xn = [_rms(hh, pre_g_ref[...]).astype(BF16) for hh in h]
    for c in range(d_ff // FFN_CHUNK):
        lo = c * FFN_CHUNK
        for p, x in zip(parts, xn):
            g = _dot(x, w_in_ref[:, lo:lo + FFN_CHUNK])
            u = _dot(x, w_in_ref[:, d_ff + lo:d_ff + lo + FFN_CHUNK])
            act_ref[p, lo:lo + FFN_CHUNK] = (g * jax.nn.sigmoid(g) * u).astype(BF16)
    f = [_dot(act_ref[p, :], w_down_ref[...]) for p in parts]
    if with_ple:
        proj = [_dot(p_ref[p, :].astype(BF16), wp_ref[...]) for p in parts]
    h = [hh + 0.5 * _rms(ff, post_g_ref[...]) for hh, ff in zip(h, f)]
    if with_ple:
        xn = [_rms(hh, ple_pre_g_ref[...]).astype(BF16) for hh in h]
        gate = [jax.nn.sigmoid(_dot(x, wg_ref[...])) for x in xn]
        h = [hh + _rms(gg * pp, ple_post_g_ref[...]) for hh, gg, pp in zip(h, gate, proj)]
    for p, hh in zip(parts, h):
        o_ref[p, :] = hh


def _ffn(h, layer, pre_g, w_in, w_down, post_g, proj=None, ple=None):
    t, d = h.shape
    d_ff = w_down.shape[1]
    tm = min(TM_FFN if (proj is not None or ple is not None) else 2 * TM_FFN, t)
    row = pl.BlockSpec((tm, d), lambda i: (i, 0))
    args, specs = [h], [row]
    if proj is not None:
        o_in, (lw, w_o), (lg, mix_g) = proj
        args += [o_in, w_o, mix_g]
        specs += [pl.BlockSpec((tm, o_in.shape[1]), lambda i: (i, 0)), _layer_spec(w_o, lw),
                  _layer_spec(mix_g, lg)]
    stacked = [pre_g, w_in, w_down, post_g]
    args += stacked
    specs += [_layer_spec(a, layer) for a in stacked]
    if ple is not None:
        args += list(ple)
        specs += [pl.BlockSpec((None, tm, ple[0].shape[2]), lambda i: (layer, i, 0))]
        specs += [_layer_spec(a, layer) for a in ple[1:]]
    return pl.pallas_call(
        functools.partial(_ffn_body, d_ff=d_ff, with_proj=proj is not None, with_ple=ple is not None),
        grid=(t // tm,),
        in_specs=specs,
        out_specs=row,
        out_shape=jax.ShapeDtypeStruct((t, d), F32),
        scratch_shapes=[pltpu.VMEM((tm, d_ff), BF16)],
        compiler_params=_params(1),
        name="ffn",
    )(*args)


def _mla_proj_body(h_ref, pos_ref, inv_ref, pre_g_ref, w_in_ref, qg_ref, kvg_ref,
                   wq_t_ref, wk_ref, wv_t_ref, q_t_ref, k_ref, v_t_ref, *, q_scale):
    tm = h_ref.shape[0]
    ang = inv_ref[...] * pos_ref[...].astype(F32)
    cos32 = jnp.cos(ang)
    sin32 = jnp.sin(ang)
    pad = jnp.zeros((LANES - MLA_ROPE, tm), F32)
    cos_t = jnp.concatenate([cos32, cos32, pad], axis=0)
    sin_t = jnp.concatenate([-sin32, sin32, pad], axis=0)
    cos_n = cos_t.T
    sin_n = sin_t.T
    hn = _rms(h_ref[...], pre_g_ref[...]).astype(BF16)
    c = _dot(hn, w_in_ref[...])
    cq = _rms(c[:, :MLA_Q_LORA], qg_ref[...])
    ckv = _rms(c[:, MLA_Q_LORA:MLA_Q_LORA + MLA_KV_LORA], kvg_ref[...])
    o = MLA_Q_LORA + MLA_KV_LORA
    one = jnp.where(lax.broadcasted_iota(jnp.int32, (1, LANES), 1) == OFFSET_ROW - MLA_NOPE, 1.0, 0.0)
    kr = (c[:, o:o + LANES] * cos_n + c[:, o + LANES:o + 2 * LANES] * sin_n + one).astype(BF16)
    cq_t = cq.T.astype(BF16)
    ckv_t = ckv.T.astype(BF16)
    kn = _dot(ckv.astype(BF16), wk_ref[...])
    v_t = _dot(wv_t_ref[...], ckv_t)
    for hd in range(MLA_HEADS):
        k_ref[:, hd * MLA_QK_PAD:hd * MLA_QK_PAD + MLA_NOPE] = (
            kn[:, hd * MLA_NOPE:(hd + 1) * MLA_NOPE].astype(BF16))
        k_ref[:, hd * MLA_QK_PAD + MLA_NOPE:(hd + 1) * MLA_QK_PAD] = kr
        v_t_ref[hd] = v_t[hd * MLA_V:(hd + 1) * MLA_V, :].astype(BF16)
        qh = _dot(wq_t_ref[hd], cq_t)
        qr = qh[MLA_NOPE:MLA_NOPE + LANES] * cos_t + qh[MLA_NOPE + LANES:] * sin_t
        q_t_ref[hd, 0:MLA_NOPE, :] = (qh[:MLA_NOPE] * q_scale).astype(BF16)
        q_t_ref[hd, MLA_NOPE:MLA_QK_PAD, :] = (qr * q_scale).astype(BF16)


def _mla_weights(w_in, w_uq, w_ukv):
    n = w_in.shape[0]
    half = MLA_ROPE // 2

    def rope_slabs(w):
        x1, x2 = w[..., :half], w[..., half:]
        z = jnp.zeros(w.shape[:-1] + (LANES - MLA_ROPE,), w.dtype)
        return jnp.concatenate([x1, x2, z, x2, x1, z], axis=-1)

    o = MLA_Q_LORA + MLA_KV_LORA
    w_in_k = jnp.concatenate([w_in[..., :o], rope_slabs(w_in[..., o:])], axis=-1).astype(BF16)
    wq = w_uq.reshape(n, MLA_Q_LORA, MLA_HEADS, MLA_QK)
    wq = jnp.concatenate([wq[..., :MLA_NOPE], rope_slabs(wq[..., MLA_NOPE:])], axis=-1)
    wq_t = jnp.transpose(wq, (0, 2, 3, 1)).astype(BF16)
    wkv = w_ukv.reshape(n, MLA_KV_LORA, MLA_HEADS, MLA_NOPE + MLA_V)
    wk = wkv[..., :MLA_NOPE].reshape(n, MLA_KV_LORA, MLA_HEADS * MLA_NOPE).astype(BF16)
    wv = wkv[..., MLA_NOPE:].reshape(n, MLA_KV_LORA, MLA_HEADS * MLA_V)
    wv_t = jnp.swapaxes(wv, 1, 2).astype(BF16)
    return w_in_k, wq_t, wk, wv_t


def _mla_proj(h, pos, layer, j, pre_g, weights, q_norm_g, kv_norm_g, qc):
    t, d = h.shape
    half = MLA_ROPE // 2
    inv = 1.0 / (ROPE_BASE ** (jnp.arange(0, MLA_ROPE, 2, dtype=F32) / MLA_ROPE))
    q_scale = (MLA_QK ** -0.5) * math.log2(math.e)
    w_in_k, wq_t, wk, wv_t = weights
    n = t // qc
    return pl.pallas_call(
        functools.partial(_mla_proj_body, q_scale=q_scale),
        grid=(n,),
        in_specs=[pl.BlockSpec((qc, d), lambda i: (i, 0)), pl.BlockSpec((1, qc), lambda i: (0, i)),
                  _const_spec((half, 1)), _layer_spec(pre_g, layer), _layer_spec(w_in_k, j),
                  _layer_spec(q_norm_g, j), _layer_spec(kv_norm_g, j),
                  _layer_spec(wq_t, j), _layer_spec(wk, j), _layer_spec(wv_t, j)],
        out_specs=[pl.BlockSpec((None, MLA_HEADS, MLA_QK_PAD, qc), lambda i: (i, 0, 0, 0)),
                   pl.BlockSpec((qc, MLA_HEADS * MLA_QK_PAD), lambda i: (i, 0)),
                   pl.BlockSpec((None, MLA_HEADS, MLA_V, qc), lambda i: (i, 0, 0, 0))],
        out_shape=[jax.ShapeDtypeStruct((n, MLA_HEADS, MLA_QK_PAD, qc), BF16),
                   jax.ShapeDtypeStruct((t, MLA_HEADS * MLA_QK_PAD), BF16),
                   jax.ShapeDtypeStruct((n, MLA_HEADS, MLA_V, qc), BF16)],
        compiler_params=_params(1),
        name="mla_proj",
    )(h, pos.reshape(1, t), inv.reshape(half, 1), pre_g, w_in_k, q_norm_g, kv_norm_g, wq_t, wk, wv_t)


def _qk_t(k, q_t, mask):
    s_t = _dot(k, q_t)
    if mask is not None:
        s_t = jnp.where(mask, s_t, -jnp.inf)
    return s_t


def _softmax_pv_t(s_t, v_t, m_ref, l_ref, acc_ref, r):
    m_prev = m_ref[r]
    m_new = jnp.maximum(m_prev, jnp.max(s_t, axis=0, keepdims=True))
    alpha = jnp.exp2(m_prev - m_new)
    p_t = jnp.exp2(s_t - m_new)
    l_ref[r] = alpha * l_ref[r] + jnp.sum(p_t, axis=0, keepdims=True)
    acc_ref[r] = _dot(v_t, p_t.astype(BF16)) + alpha * acc_ref[r]
    m_ref[r] = m_new


def _lagged_pv_t(s_t, v_t, m_ref, l_ref, acc_ref, qa_ref, excess_ref, r):
    p_t = jnp.exp2(s_t)
    top = jnp.max(s_t, axis=0, keepdims=True)
    l_new = l_ref[r] + jnp.sum(p_t, axis=0, keepdims=True)
    acc_new = _dot(v_t, p_t.astype(BF16)) + acc_ref[r]
    m_old = m_ref[r]
    m_new = (m_old + jnp.maximum(top, 0.0)).astype(BF16).astype(F32)
    alpha = jnp.exp2(m_old - m_new)
    l_ref[r] = l_new * alpha
    acc_ref[r] = acc_new * alpha
    m_ref[r] = m_new
    qa_ref[r, OFFSET_ROW:OFFSET_ROW + 1, :] = (-m_new).astype(BF16)
    excess_ref[...] = jnp.maximum(excess_ref[...], top)


def _attn_body(q_t_ref, k_ref, v_t_ref, o_ref, m_ref, l_ref, acc_ref, qa_ref, excess_ref,
               *, tq, tk, qc, unroll):
    i = pl.program_id(2)
    n_sub = tq // qc
    kc = tk // qc
    n_blocks = i * (tq // tk) // unroll
    d0 = pl.multiple_of(i * tq, tq)
    krow = lax.broadcasted_iota(jnp.int32, (qc, qc), 0)
    qcol = lax.broadcasted_iota(jnp.int32, (qc, qc), 1)
    tri = krow <= qcol

    def reset():
        m_ref[...] = jnp.full(m_ref.shape, -jnp.inf, F32)
        l_ref[...] = jnp.zeros(l_ref.shape, F32)
        acc_ref[...] = jnp.zeros(acc_ref.shape, F32)

    def exact(s_t, v_t, r):
        _softmax_pv_t(s_t, v_t, m_ref, l_ref, acc_ref, r)

    def lagged(s_t, v_t, r):
        _lagged_pv_t(s_t, v_t, m_ref, l_ref, acc_ref, qa_ref, excess_ref, r)

    def pipelined(items, update):
        s_t = items[0][0]()
        for n, (_, v_t, r) in enumerate(items):
            more = n + 1 < len(items)
            ahead = more and items[n + 1][2] != r
            s_next = items[n + 1][0]() if ahead else None
            update(s_t, v_t, r)
            s_t = items[n + 1][0]() if more and not ahead else s_next

    def tile_keys(c0, c1):
        k = k_ref[pl.ds(pl.multiple_of(d0 + c0 * qc, qc), (c1 - c0) * qc), :]
        v_t = jnp.concatenate([v_t_ref[i * n_sub + c] for c in range(c0, c1)], axis=1)
        return k, v_t

    def block_keys(jb):
        k = k_ref[pl.ds(pl.multiple_of(jb * tk, tk), tk), :]
        v_t = jnp.concatenate([v_t_ref[jb * kc + c] for c in range(kc)], axis=1)
        return k, v_t

    def block_items(j, q_ref):
        items = []
        for u in range(unroll):
            k, v_t = block_keys(j * unroll + u)
            items += [(lambda k=k, r=r: _qk_t(k, q_ref[r], None), v_t, r) for r in range(n_sub)]
        return items

    def finish():
        for r in range(n_sub):
            o_t = acc_ref[r] / l_ref[r]
            o_ref[r * qc:(r + 1) * qc, :] = o_t.T.astype(o_ref.dtype)

    reset()
    own = []
    for r in range(n_sub):
        k, v_t = tile_keys(r, r + 1)
        own.append((lambda k=k, r=r: _qk_t(k, q_t_ref[r], tri), v_t, r))
    pipelined(own, exact)

    for r in range(n_sub):
        m = m_ref[r]
        m_b = m.astype(BF16).astype(F32)
        shift = jnp.exp2(m - m_b)
        l_ref[r] = l_ref[r] * shift
        acc_ref[r] = acc_ref[r] * shift
        m_ref[r] = m_b
        qa_ref[r] = q_t_ref[r]
        qa_ref[r, OFFSET_ROW:OFFSET_ROW + 1, :] = (-m_b).astype(BF16)
    excess_ref[...] = jnp.zeros(excess_ref.shape, F32)

    rest = []
    for r in range(1, n_sub):
        for c0 in range(0, r, kc):
            k, v_t = tile_keys(c0, min(c0 + kc, r))
            rest.append((lambda k=k, r=r: _qk_t(k, qa_ref[r], None), v_t, r))
    if rest:
        pipelined(rest, lagged)

    def lagged_body(j, carry):
        pipelined(block_items(j, qa_ref), lagged)
        return carry

    lax.fori_loop(0, n_blocks, lagged_body, 0)
    finish()

    @pl.when(jnp.max(excess_ref[...]) > FAST_LIMIT)
    def _():
        reset()

        def exact_body(j, carry):
            pipelined(block_items(j, q_t_ref), exact)
            return carry

        lax.fori_loop(0, n_blocks, exact_body, 0)
        diag = []
        for r in range(n_sub):
            k, v_t = tile_keys(0, r + 1)
            mask = tri if r == 0 else jnp.concatenate([jnp.ones((r * qc, qc), jnp.bool_), tri], axis=0)
            diag.append((lambda k=k, r=r, mask=mask: _qk_t(k, q_t_ref[r], mask), v_t, r))
        pipelined(diag, exact)
        finish()


def _attn(q_t, k, v_t, b, s, qc):
    tq = min(TQ_ATTN, s)
    tk = min(TK_ATTN, tq)
    n_sub = tq // qc
    unroll = math.gcd(KB_UNROLL, tq // tk)
    return pl.pallas_call(
        functools.partial(_attn_body, tq=tq, tk=tk, qc=qc, unroll=unroll),
        grid=(b, MLA_HEADS, s // tq),
        in_specs=[pl.BlockSpec((n_sub, None, MLA_QK_PAD, qc),
                               lambda bi, h, i: (bi * (s // tq) + i, h, 0, 0)),
                  pl.BlockSpec((None, s, MLA_QK_PAD), lambda bi, h, i: (bi, 0, h)),
                  pl.BlockSpec((s // qc, None, MLA_V, qc), lambda bi, h, i: (bi, h, 0, 0))],
        out_specs=pl.BlockSpec((None, tq, MLA_V), lambda bi, h, i: (bi, i, h)),
        out_shape=jax.ShapeDtypeStruct((b, s, MLA_HEADS * MLA_V), BF16),
        scratch_shapes=[pltpu.VMEM((n_sub, 1, qc), F32), pltpu.VMEM((n_sub, 1, qc), F32),
                        pltpu.VMEM((n_sub, MLA_V, qc), F32),
                        pltpu.VMEM((n_sub, MLA_QK_PAD, qc), BF16),
                        pltpu.VMEM((1, qc), F32)],
        compiler_params=_params(3),
        name="attn",
    )(q_t, k.reshape(b, s, MLA_HEADS * MLA_QK_PAD), v_t)


def _softplus(x):
    return jnp.maximum(x, 0.0) + jnp.log1p(jnp.exp(-jnp.abs(x)))


def _hyb_body(h_ref, pre_g_ref, w_in_ref, ln_g_ref, ln_b_ref, ws_ref, bs_ref,
              conv_w_ref, conv_b_ref, dt_bias_ref, a_log_ref, dskip_ref, ng_ref,
              w_out_ref, post_g_ref, o_ref,
              ext_ref, state_ref, y_ref, xs_ref, b_ref, c_ref, z_ref, dt_ref, da_ref, *, tm, n_split):
    gr = tm // n_split
    nc = gr // CHUNK
    t_idx = pl.program_id(1)

    @pl.when(t_idx == 0)
    def _():
        ext_ref[0:CONV_HALO, :] = jnp.zeros((CONV_HALO, SSD_CONV_CH), F32)
        state_ref[...] = jnp.zeros(state_ref.shape, F32)

    row = lax.broadcasted_iota(jnp.int32, (CHUNK, CHUNK), 0)
    col = lax.broadcasted_iota(jnp.int32, (CHUNK, CHUNK), 1)
    causal = col <= row

    def in_proj(r0):
        rows = slice(r0, r0 + gr)
        h = h_ref[rows, :]
        hn = _rms(h, pre_g_ref[...]).astype(BF16)
        u_pre = _dot(hn, w_in_ref[:, 0:GM_WIDTH])
        v_pre = _dot(hn, w_in_ref[:, GM_WIDTH:2 * GM_WIDTH])
        o = 2 * GM_WIDTH
        z_ref[rows, :] = _dot(hn, w_in_ref[:, o:o + SSD_INNER])
        o += SSD_INNER
        ext_ref[CONV_HALO + r0:CONV_HALO + r0 + gr, :] = _dot(hn, w_in_ref[:, o:o + SSD_CONV_CH])
        o += SSD_CONV_CH
        dt_pre = _dot(hn, w_in_ref[:, o:o + LANES])
        return h, u_pre, v_pre, dt_pre

    def gmlp(r0, u_pre, v_pre):
        u = jax.nn.gelu(u_pre)
        v = jax.nn.gelu(v_pre)
        for hd in range(GM_HEADS):
            sl = slice(hd * GM_HEAD_DIM, (hd + 1) * GM_HEAD_DIM)
            vh = v[:, sl]
            mu = jnp.mean(vh, axis=-1, keepdims=True)
            xc = vh - mu
            var = jnp.mean(xc * xc, axis=-1, keepdims=True)
            vn = (xc * lax.rsqrt(var + LN_EPS) * ln_g_ref[:, sl] + ln_b_ref[:, sl]).astype(BF16)
            rhs = jnp.concatenate([vn[c * CHUNK:(c + 1) * CHUNK] for c in range(nc)], axis=1)
            w = jnp.where(causal, ws_ref[hd], 0.0).astype(BF16)
            mixed = _dot(w, rhs)
            bias = bs_ref[:, hd:hd + 1]
            for c in range(nc):
                y_ref[r0 + c * CHUNK:r0 + (c + 1) * CHUNK, sl] = (
                    u[c * CHUNK:(c + 1) * CHUNK, sl] * (mixed[:, c * CHUNK:(c + 1) * CHUNK] + bias)
                ).astype(BF16)

    def ssd_front(r0, dt_pre):
        rows = slice(r0, r0 + gr)
        dt = _softplus(dt_pre + dt_bias_ref[...])
        dt_ref[rows, :] = dt
        da_ref[rows, :] = dt * (-jnp.exp(a_log_ref[...]))
        conv = conv_b_ref[...]
        for kk in range(SSD_CONV):
            lo = r0 + CONV_HALO - (SSD_CONV - 1) + kk
            conv = conv + conv_w_ref[kk:kk + 1, :] * ext_ref[lo:lo + gr, :]
        xbc = conv * jax.nn.sigmoid(conv)
        xs_ref[rows, :] = xbc[:, :SSD_INNER]
        b_ref[rows, :] = xbc[:, SSD_INNER:SSD_INNER + SSD_GROUPS * SSD_STATE]
        c_ref[rows, :] = xbc[:, SSD_INNER + SSD_GROUPS * SSD_STATE:]

    tril = jnp.where(causal, 1.0, 0.0).astype(F32)
    lane = lax.broadcasted_iota(jnp.int32, (CHUNK, LANES), 1)
    first_head = lane < SSD_HEAD_DIM

    def chunk_body(c, carry):
        rows = pl.ds(c * CHUNK, CHUNK)
        dt_c = dt_ref[rows, :]
        acs = jnp.dot(tril, da_ref[rows, :], precision=lax.Precision.HIGHEST,
                      preferred_element_type=F32)
        acs_t = acs.T
        bc = b_ref[rows, :]
        cc = c_ref[rows, :]
        for g in range(SSD_GROUPS):
            gs = slice(g * SSD_STATE, (g + 1) * SSD_STATE)
            bg_t = bc[:, gs].T.astype(BF16)
            cg = cc[:, gs].astype(BF16)
            cb = _dot(cg, bg_t)
            ys = []
            for pj in range(SSD_PAIRS // SSD_GROUPS):
                j = g * (SSD_PAIRS // SSD_GROUPS) + pj
                ha, hb = 2 * j, 2 * j + 1
                sl = slice(j * LANES, (j + 1) * LANES)
                acol_a = jnp.broadcast_to(acs[:, ha:ha + 1], (CHUNK, LANES))
                acol_b = jnp.broadcast_to(acs[:, hb:hb + 1], (CHUNK, LANES))
                arow_a = jnp.broadcast_to(acs_t[ha:ha + 1, :], (CHUNK, CHUNK))
                arow_b = jnp.broadcast_to(acs_t[hb:hb + 1, :], (CHUNK, CHUNK))
                dtw = jnp.where(first_head, jnp.broadcast_to(dt_c[:, ha:ha + 1], (CHUNK, LANES)),
                                jnp.broadcast_to(dt_c[:, hb:hb + 1], (CHUNK, LANES)))
                acw = jnp.where(first_head, acol_a, acol_b)
                xs_p = xs_ref[rows, sl]
                xd = xs_p * dtw
                xd_bf = xd.astype(BF16)
                m_a = (cb * jnp.exp(jnp.where(causal, acol_a - arow_a, -jnp.inf))).astype(BF16)
                m_b = (cb * jnp.exp(jnp.where(causal, acol_b - arow_b, -jnp.inf))).astype(BF16)
                y_diag = jnp.where(first_head, _dot(m_a, xd_bf), _dot(m_b, xd_bf))
                acw_last = acw[CHUNK - 1:CHUNK, :]
                xw = (xd * jnp.exp(acw_last - acw)).astype(BF16)
                st_new = _dot(bg_t, xw)
                prev = state_ref[:, sl]
                y_off = _dot(cg, prev.astype(BF16)) * jnp.exp(acw)
                state_ref[:, sl] = prev * jnp.exp(acw_last) + st_new
                zz = z_ref[rows, sl]
                y = y_diag + y_off + dskip_ref[:, sl] * xs_p
                ys.append(y * (zz * jax.nn.sigmoid(zz)))
            yg = jnp.concatenate(ys, axis=1)
            gw = SSD_INNER // SSD_GROUPS
            yn = _rms(yg, ng_ref[:, g * gw:(g + 1) * gw])
            y_ref[rows, GM_WIDTH + g * gw:GM_WIDTH + (g + 1) * gw] = yn.astype(BF16)
        return carry

    def out_proj(r0, h):
        rows = slice(r0, r0 + gr)
        mixed = _dot(y_ref[rows, :], w_out_ref[...])
        o_ref[rows, :] = h + _rms(mixed, post_g_ref[...])

    starts = [g * gr for g in range(n_split)]
    staged = [in_proj(r0) for r0 in starts]
    for r0, (_, u_pre, v_pre, dt_pre) in zip(starts, staged):
        gmlp(r0, u_pre, v_pre)
        ssd_front(r0, dt_pre)
    ext_ref[0:CONV_HALO, :] = ext_ref[tm:tm + CONV_HALO, :]
    for r0, (h, _, _, _) in zip(starts, staged):
        for c in range(nc):
            chunk_body(r0 // CHUNK + c, 0)
        out_proj(r0, h)


def _hyb_weights(w_in, ln_g, ln_b, w_s, b_s, conv_w, conv_b, dt_bias, a_log, d_skip, norm_g, w_out):
    n = w_in.shape[0]
    pad = (-w_in.shape[2]) % LANES
    w_in_k = jnp.pad(w_in, ((0, 0), (0, 0), (0, pad))).astype(BF16)
    row = lambda a: a.reshape(n, 1, -1)
    pad_h = lambda a: jnp.pad(row(a), ((0, 0), (0, 0), (0, LANES - SSD_HEADS)))
    return (w_in_k, row(ln_g), row(ln_b), w_s, jnp.swapaxes(b_s, 1, 2), conv_w, row(conv_b),
            pad_h(dt_bias), pad_h(a_log), row(jnp.repeat(d_skip, SSD_HEAD_DIM, axis=1)), row(norm_g),
            w_out.astype(BF16))


def _hybrid(h, layer, j, pre_g, weights, post_g):
    b, s, d = h.shape
    tm = min(TM_HYB, s)
    row = pl.BlockSpec((None, tm, d), lambda bi, i: (bi, i, 0))
    return pl.pallas_call(
        functools.partial(_hyb_body, tm=tm, n_split=math.gcd(HYB_SPLIT, tm // CHUNK)),
        grid=(b, s // tm),
        in_specs=([row, _layer_spec(pre_g, layer)] + [_layer_spec(w, j) for w in weights]
                  + [_layer_spec(post_g, layer)]),
        out_specs=row,
        out_shape=jax.ShapeDtypeStruct((b, s, d), F32),
        scratch_shapes=[
            pltpu.VMEM((tm + CONV_HALO, SSD_CONV_CH), F32),
            pltpu.VMEM((SSD_STATE, SSD_INNER), F32),
            pltpu.VMEM((tm, GM_WIDTH + SSD_INNER), BF16),
            pltpu.VMEM((tm, SSD_INNER), F32),
            pltpu.VMEM((tm, SSD_GROUPS * SSD_STATE), F32),
            pltpu.VMEM((tm, SSD_GROUPS * SSD_STATE), F32),
            pltpu.VMEM((tm, SSD_INNER), F32),
            pltpu.VMEM((tm, LANES), F32),
            pltpu.VMEM((tm, LANES), F32),
        ],
        compiler_params=_params(2),
        name="hybrid",
    )(h, pre_g, *weights, post_g)


def kernel(x, p, positions, ffn1_pre_g, ffn1_w_in, ffn1_w_down, ffn1_post_g, mix_pre_g, mix_post_g, ffn2_pre_g, ffn2_w_in, ffn2_w_down, ffn2_post_g, ple_pre_g, ple_w_gate, ple_w_proj, ple_post_g, hyb_w_in, gm_ln_g, gm_ln_b, gm_w_s, gm_b_s, ssd_conv_w, ssd_conv_b, ssd_dt_bias, ssd_a_log, ssd_d, ssd_norm_g, hyb_w_out, mla_w_in, mla_q_norm_g, mla_kv_norm_g, mla_w_uq, mla_w_ukv, mla_w_out):
    b, s, d = x.shape
    t = b * s
    depth = ffn1_w_in.shape[0]
    g3 = lambda a: a.reshape(a.shape[0], 1, -1)
    bf = lambda a: a.astype(BF16)
    ffn1 = (g3(ffn1_pre_g), bf(ffn1_w_in), bf(ffn1_w_down), g3(ffn1_post_g))
    ffn2 = (g3(ffn2_pre_g), bf(ffn2_w_in), bf(ffn2_w_down), g3(ffn2_post_g))
    ple = (p.reshape(depth, t, -1), g3(ple_pre_g), bf(ple_w_gate), bf(ple_w_proj), g3(ple_post_g))
    mix_pre, mix_post = g3(mix_pre_g), g3(mix_post_g)
    hyb_w = _hyb_weights(hyb_w_in, gm_ln_g, gm_ln_b, gm_w_s, gm_b_s, ssd_conv_w, ssd_conv_b,
                         ssd_dt_bias, ssd_a_log, ssd_d, ssd_norm_g, hyb_w_out)
    mla_w = _mla_weights(mla_w_in, mla_w_uq, mla_w_ukv)
    mla_w_o = bf(mla_w_out)
    qc = min(QC_ATTN, s)
    h = x.reshape(t, d)
    for i in range(depth):
        j = i // 2
        h = _ffn(h, i, *ffn1)
        if i % 2 == 0:
            h = _hybrid(h.reshape(b, s, d), i, j, mix_pre, hyb_w, mix_post).reshape(t, d)
            h = _ffn(h, i, *ffn2, ple=ple)
        else:
            q_t, k, v_t = _mla_proj(h, positions, i, j, mix_pre, mla_w, g3(mla_q_norm_g),
                                    g3(mla_kv_norm_g), qc)
            o = _attn(q_t, k, v_t, b, s, qc)
            h = _ffn(h, i, *ffn2, proj=(o.reshape(t, -1), (j, mla_w_o), (i, mix_post)), ple=ple)
    return h.reshape(b, s, d)
```

```python
import functools
import math

import jax
import jax.numpy as jnp
from jax import lax
from jax.experimental import pallas as pl
from jax.experimental.pallas import tpu as pltpu

F32 = jnp.float32
BF16 = jnp.bfloat16

NORM_EPS = 1e-6
LN_EPS = 1e-5

LANES = 128
CHUNK = 128

GM_HEADS = 8
GM_HEAD_DIM = 128
GM_WIDTH = GM_HEADS * GM_HEAD_DIM

SSD_HEAD_DIM = 64
SSD_HEADS = 16
SSD_INNER = SSD_HEADS * SSD_HEAD_DIM
SSD_GROUPS = 2
SSD_STATE = 128
SSD_CONV = 4
SSD_CONV_CH = SSD_INNER + 2 * SSD_GROUPS * SSD_STATE
SSD_PAIRS = SSD_HEADS // 2
CONV_HALO = 8

MLA_HEADS = 16
MLA_NOPE = 128
MLA_ROPE = 64
MLA_V = 128
MLA_Q_LORA = 256
MLA_KV_LORA = 128
MLA_QK = MLA_NOPE + MLA_ROPE
MLA_QK_PAD = 256
ROPE_BASE = 10000.0
OFFSET_ROW = MLA_QK
FAST_LIMIT = 60.0

FFN_CHUNK = 256
FFN_GROUP = 256
HYB_SPLIT = 2
VMEM_LIMIT = 56 * 1024 * 1024

TM_FFN = 512
TM_HYB = 512
QC_ATTN = 512
TQ_ATTN = 2048
TK_ATTN = 1024
KB_UNROLL = 2


def _rms(x, g):
    return x * lax.rsqrt(jnp.mean(x * x, axis=-1, keepdims=True) + NORM_EPS) * g


def _dot(a, b):
    return jnp.dot(a, b, preferred_element_type=F32)


def _const_spec(shape):
    nd = len(shape)
    return pl.BlockSpec(shape, lambda *_: (0,) * nd, pipeline_mode=pl.Buffered(1))


def _layer_spec(stacked, layer):
    zeros = (0,) * (stacked.ndim - 1)
    return pl.BlockSpec((None,) + stacked.shape[1:], lambda *_: (layer,) + zeros,
                        pipeline_mode=pl.Buffered(1))


def _params(n_axes):
    return pltpu.CompilerParams(
        dimension_semantics=("arbitrary",) * n_axes, vmem_limit_bytes=VMEM_LIMIT)


def _ffn_body(*refs, d_ff, with_proj, with_ple):
    refs = list(refs)
    h_ref = refs.pop(0)
    tm = h_ref.shape[0]
    grp = min(FFN_GROUP, tm)
    parts = [slice(r0, r0 + grp) for r0 in range(0, tm, grp)]
    h = [h_ref[p, :] for p in parts]
    if with_proj:
        o_in_ref, w_o_ref, mix_g_ref = refs[:3]
        del refs[:3]
        mixed = [_dot(o_in_ref[p, :], w_o_ref[...]) for p in parts]
        h = [hh + _rms(mm, mix_g_ref[...]) for hh, mm in zip(h, mixed)]
    pre_g_ref, w_in_ref, w_down_ref, post_g_ref = refs[:4]
    del refs[:4]
    if with_ple:
        p_ref, ple_pre_g_ref, wg_ref, wp_ref, ple_post_g_ref = refs[:5]
        del refs[:5]
    o_ref, act_ref = refs

    xn = [_rms(hh, pre_g_ref[...]).astype(BF16) for hh in h]
    for c in range(d_ff // FFN_CHUNK):
        lo = c * FFN_CHUNK
        for p, x in zip(parts, xn):
            g = _dot(x, w_in_ref[:, lo:lo + FFN_CHUNK])
            u = _dot(x, w_in_ref[:, d_ff + lo:d_ff + lo + FFN_CHUNK])
            act_ref[p, lo:lo + FFN_CHUNK] = (g * jax.nn.sigmoid(g) * u).astype(BF16)
    f = [_dot(act_ref[p, :], w_down_ref[...]) for p in parts]
    if with_ple:
        proj = [_dot(p_ref[p, :].astype(BF16), wp_ref[...]) for p in parts]
    h = [hh + 0.5 * _rms(ff, post_g_ref[...]) for hh, ff in zip(h, f)]
    if with_ple:
        xn = [_rms(hh, ple_pre_g_ref[...]).astype(BF16) for hh in h]
        gate = [jax.nn.sigmoid(_dot(x, wg_ref[...])) for x in xn]
        h = [hh + _rms(gg * pp, ple_post_g_ref[...]) for hh, gg, pp in zip(h, gate, proj)]
    for p, hh in zip(parts, h):
        o_ref[p, :] = hh


def _ffn(h, layer, pre_g, w_in, w_down, post_g, proj=None, ple=None):
    t, d = h.shape
    d_ff = w_down.shape[1]
    tm = min(TM_FFN if (proj is not None or ple is not None) else 2 * TM_FFN, t)
    row = pl.BlockSpec((tm, d), lambda i: (i, 0))
    args, specs = [h], [row]
    if proj is not None:
        o_in, (lw, w_o), (lg, mix_g) = proj
        args += [o_in, w_o, mix_g]
        specs += [pl.BlockSpec((tm, o_in.shape[1]), lambda i: (i, 0)), _layer_spec(w_o, lw),
                  _layer_spec(mix_g, lg)]
    stacked = [pre_g, w_in, w_down, post_g]
    args += stacked
    specs += [_layer_spec(a, layer) for a in stacked]
    if ple is not None:
        args += list(ple)
        specs += [pl.BlockSpec((None, tm, ple[0].shape[2]), lambda i: (layer, i, 0))]
        specs += [_layer_spec(a, layer) for a in ple[1:]]
    return pl.pallas_call(
        functools.partial(_ffn_body, d_ff=d_ff, with_proj=proj is not None, with_ple=ple is not None),
        grid=(t // tm,),
        in_specs=specs,
        out_specs=row,
        out_shape=jax.ShapeDtypeStruct((t, d), F32),
        scratch_shapes=[pltpu.VMEM((tm, d_ff), BF16)],
        compiler_params=_params(1),
        name="ffn",
    )(*args)


def _mla_proj_body(h_ref, pos_ref, inv_ref, pre_g_ref, w_in_ref, qg_ref, kvg_ref,
                   wq_t_ref, wk_ref, wv_t_ref, q_t_ref, k_ref, v_t_ref, *, q_scale):
    tm = h_ref.shape[0]
    ang = inv_ref[...] * pos_ref[...].astype(F32)
    cos32 = jnp.cos(ang)
    sin32 = jnp.sin(ang)
    pad = jnp.zeros((LANES - MLA_ROPE, tm), F32)
    cos_t = jnp.concatenate([cos32, cos32, pad], axis=0)
    sin_t = jnp.concatenate([-sin32, sin32, pad], axis=0)
    cos_n = cos_t.T
    sin_n = sin_t.T
    hn = _rms(h_ref[...], pre_g_ref[...]).astype(BF16)
    c = _dot(hn, w_in_ref[...])
    cq = _rms(c[:, :MLA_Q_LORA], qg_ref[...])
    ckv = _rms(c[:, MLA_Q_LORA:MLA_Q_LORA + MLA_KV_LORA], kvg_ref[...])
    o = MLA_Q_LORA + MLA_KV_LORA
    one = jnp.where(lax.broadcasted_iota(jnp.int32, (1, LANES), 1) == OFFSET_ROW - MLA_NOPE, 1.0, 0.0)
    kr = (c[:, o:o + LANES] * cos_n + c[:, o + LANES:o + 2 * LANES] * sin_n + one).astype(BF16)
    cq_t = cq.T.astype(BF16)
    ckv_t = ckv.T.astype(BF16)
    kn = _dot(ckv.astype(BF16), wk_ref[...])
    v_t = _dot(wv_t_ref[...], ckv_t)
    for hd in range(MLA_HEADS):
        k_ref[:, hd * MLA_QK_PAD:hd * MLA_QK_PAD + MLA_NOPE] = (
            kn[:, hd * MLA_NOPE:(hd + 1) * MLA_NOPE].astype(BF16))
        k_ref[:, hd * MLA_QK_PAD + MLA_NOPE:(hd + 1) * MLA_QK_PAD] = kr
        v_t_ref[hd] = v_t[hd * MLA_V:(hd + 1) * MLA_V, :].astype(BF16)
        qh = _dot(wq_t_ref[hd], cq_t)
        qr = qh[MLA_NOPE:MLA_NOPE + LANES] * cos_t + qh[MLA_NOPE + LANES:] * sin_t
        q_t_ref[hd, 0:MLA_NOPE, :] = (qh[:MLA_NOPE] * q_scale).astype(BF16)
        q_t_ref[hd, MLA_NOPE:MLA_QK_PAD, :] = (qr * q_scale).astype(BF16)


def _mla_weights(w_in, w_uq, w_ukv):
    n = w_in.shape[0]
    half = MLA_ROPE // 2

    def rope_slabs(w):
        x1, x2 = w[..., :half], w[..., half:]
        z = jnp.zeros(w.shape[:-1] + (LANES - MLA_ROPE,), w.dtype)
        return jnp.concatenate([x1, x2, z, x2, x1, z], axis=-1)

    o = MLA_Q_LORA + MLA_KV_LORA
    w_in_k = jnp.concatenate([w_in[..., :o], rope_slabs(w_in[..., o:])], axis=-1).astype(BF16)
    wq = w_uq.reshape(n, MLA_Q_LORA, MLA_HEADS, MLA_QK)
    wq = jnp.concatenate([wq[..., :MLA_NOPE], rope_slabs(wq[..., MLA_NOPE:])], axis=-1)
    wq_t = jnp.transpose(wq, (0, 2, 3, 1)).astype(BF16)
    wkv = w_ukv.reshape(n, MLA_KV_LORA, MLA_HEADS, MLA_NOPE + MLA_V)
    wk = wkv[..., :MLA_NOPE].reshape(n, MLA_KV_LORA, MLA_HEADS * MLA_NOPE).astype(BF16)
    wv = wkv[..., MLA_NOPE:].reshape(n, MLA_KV_LORA, MLA_HEADS * MLA_V)
    wv_t = jnp.swapaxes(wv, 1, 2).astype(BF16)
    return w_in_k, wq_t, wk, wv_t


def _mla_proj(h, pos, layer, j, pre_g, weights, q_norm_g, kv_norm_g, qc):
    t, d = h.shape
    half = MLA_ROPE // 2
    inv = 1.0 / (ROPE_BASE ** (jnp.arange(0, MLA_ROPE, 2, dtype=F32) / MLA_ROPE))
    q_scale = (MLA_QK ** -0.5) * math.log2(math.e)
    w_in_k, wq_t, wk, wv_t = weights
    n = t // qc
    return pl.pallas_call(
        functools.partial(_mla_proj_body, q_scale=q_scale),
        grid=(n,),
        in_specs=[pl.BlockSpec((qc, d), lambda i: (i, 0)), pl.BlockSpec((1, qc), lambda i: (0, i)),
                  _const_spec((half, 1)), _layer_spec(pre_g, layer), _layer_spec(w_in_k, j),
                  _layer_spec(q_norm_g, j), _layer_spec(kv_norm_g, j),
                  _layer_spec(wq_t, j), _layer_spec(wk, j), _layer_spec(wv_t, j)],
        out_specs=[pl.BlockSpec((None, MLA_HEADS, MLA_QK_PAD, qc), lambda i: (i, 0, 0, 0)),
                   pl.BlockSpec((qc, MLA_HEADS * MLA_QK_PAD), lambda i: (i, 0)),
                   pl.BlockSpec((None, MLA_HEADS, MLA_V, qc), lambda i: (i, 0, 0, 0))],
        out_shape=[jax.ShapeDtypeStruct((n, MLA_HEADS, MLA_QK_PAD, qc), BF16),
                   jax.ShapeDtypeStruct((t, MLA_HEADS * MLA_QK_PAD), BF16),
                   jax.ShapeDtypeStruct((n, MLA_HEADS, MLA_V, qc), BF16)],
        compiler_params=_params(1),
        name="mla_proj",
    )(h, pos.reshape(1, t), inv.reshape(half, 1), pre_g, w_in_k, q_norm_g, kv_norm_g, wq_t, wk, wv_t)


def _qk_t(k, q_t, mask):
    s_t = _dot(k, q_t)
    if mask is not None:
        s_t = jnp.where(mask, s_t, -jnp.inf)
    return s_t


def _softmax_pv_t(s_t, v_t, m_ref, l_ref, acc_ref, r):
    m_prev = m_ref[r]
    m_new = jnp.maximum(m_prev, jnp.max(s_t, axis=0, keepdims=True))
    alpha = jnp.exp2(m_prev - m_new)
    p_t = jnp.exp2(s_t - m_new)
    l_ref[r] = alpha * l_ref[r] + jnp.sum(p_t, axis=0, keepdims=True)
    acc_ref[r] = _dot(v_t, p_t.astype(BF16)) + alpha * acc_ref[r]
    m_ref[r] = m_new


def _lagged_pv_t(s_t, v_t, m_ref, l_ref, acc_ref, qa_ref, excess_ref, r):
    p_t = jnp.exp2(s_t)
    top = jnp.max(s_t, axis=0, keepdims=True)
    l_new = l_ref[r] + jnp.sum(p_t, axis=0, keepdims=True)
    acc_new = _dot(v_t, p_t.astype(BF16)) + acc_ref[r]
    m_old = m_ref[r]
    m_new = (m_old + jnp.maximum(top, 0.0)).astype(BF16).astype(F32)
    alpha = jnp.exp2(m_old - m_new)
    l_ref[r] = l_new * alpha
    acc_ref[r] = acc_new * alpha
    m_ref[r] = m_new
    qa_ref[r, OFFSET_ROW:OFFSET_ROW + 1, :] = (-m_new).astype(BF16)
    excess_ref[...] = jnp.maximum(excess_ref[...], top)


def _attn_body(q_t_ref, k_ref, v_t_ref, o_ref, m_ref, l_ref, acc_ref, qa_ref, excess_ref,
               *, tq, tk, qc, unroll):
    i = pl.program_id(2)
    n_sub = tq // qc
    kc = tk // qc
    n_blocks = i * (tq // tk) // unroll
    d0 = pl.multiple_of(i * tq, tq)
    krow = lax.broadcasted_iota(jnp.int32, (qc, qc), 0)
    qcol = lax.broadcasted_iota(jnp.int32, (qc, qc), 1)
    tri = krow <= qcol

    def reset():
        m_ref[...] = jnp.full(m_ref.shape, -jnp.inf, F32)
        l_ref[...] = jnp.zeros(l_ref.shape, F32)
        acc_ref[...] = jnp.zeros(acc_ref.shape, F32)

    def exact(s_t, v_t, r):
        _softmax_pv_t(s_t, v_t, m_ref, l_ref, acc_ref, r)

    def lagged(s_t, v_t, r):
        _lagged_pv_t(s_t, v_t, m_ref, l_ref, acc_ref, qa_ref, excess_ref, r)

    def pipelined(items, update):
        s_t = items[0][0]()
        for n, (_, v_t, r) in enumerate(items):
            more = n + 1 < len(items)
            ahead = more and items[n + 1][2] != r
            s_next = items[n + 1][0]() if ahead else None
            update(s_t, v_t, r)
            s_t = items[n + 1][0]() if more and not ahead else s_next

    def tile_keys(c0, c1):
        k = k_ref[pl.ds(pl.multiple_of(d0 + c0 * qc, qc), (c1 - c0) * qc), :]
        v_t = jnp.concatenate([v_t_ref[i * n_sub + c] for c in range(c0, c1)], axis=1)
        return k, v_t

    def block_keys(jb):
        k = k_ref[pl.ds(pl.multiple_of(jb * tk, tk), tk), :]
        v_t = jnp.concatenate([v_t_ref[jb * kc + c] for c in range(kc)], axis=1)
        return k, v_t

    def block_items(j, q_ref):
        items = []
        for u in range(unroll):
            k, v_t = block_keys(j * unroll + u)
            items += [(lambda k=k, r=r: _qk_t(k, q_ref[r], None), v_t, r) for r in range(n_sub)]
        return items

    def finish():
        for r in range(n_sub):
            o_t = acc_ref[r] / l_ref[r]
            o_ref[r * qc:(r + 1) * qc, :] = o_t.T.astype(o_ref.dtype)

    reset()
    excess_ref[...] = jnp.zeros(excess_ref.shape, F32)

    def own_scores(k, r):
        s_t = _qk_t(k, q_t_ref[r], None)
        m0 = s_t[0:1, :].astype(BF16).astype(F32)
        m_ref[r] = m0
        qa_ref[r] = q_t_ref[r]
        return jnp.where(tri, s_t - m0, -jnp.inf)

    diag = []
    for r in reversed(range(n_sub)):
        k, v_t = tile_keys(r, r + 1)
        diag.append((lambda k=k, r=r: own_scores(k, r), v_t, r))
    for r in reversed(range(1, n_sub)):
        for c0 in range(0, r, kc):
            k, v_t = tile_keys(c0, min(c0 + kc, r))
            diag.append((lambda k=k, r=r: _qk_t(k, qa_ref[r], None), v_t, r))
    pipelined(diag, lagged)


    def lagged_body(j, carry):
        pipelined(block_items(j, qa_ref), lagged)
        return carry

    lax.fori_loop(0, n_blocks, lagged_body, 0)
    finish()

    @pl.when(jnp.max(excess_ref[...]) > FAST_LIMIT)
    def _():
        reset()

        def exact_body(j, carry):
            pipelined(block_items(j, q_t_ref), exact)
            return carry

        lax.fori_loop(0, n_blocks, exact_body, 0)
        diag = []
        for r in range(n_sub):
            k, v_t = tile_keys(0, r + 1)
            mask = tri if r == 0 else jnp.concatenate([jnp.ones((r * qc, qc), jnp.bool_), tri], axis=0)
            diag.append((lambda k=k, r=r, mask=mask: _qk_t(k, q_t_ref[r], mask), v_t, r))
        pipelined(diag, exact)
        finish()


def _attn(q_t, k, v_t, b, s, qc):
    tq = min(TQ_ATTN, s)
    tk = min(TK_ATTN, tq)
    n_sub = tq // qc
    unroll = math.gcd(KB_UNROLL, tq // tk)
    return pl.pallas_call(
        functools.partial(_attn_body, tq=tq, tk=tk, qc=qc, unroll=unroll),
        grid=(b, MLA_HEADS, s // tq),
        in_specs=[pl.BlockSpec((n_sub, None, MLA_QK_PAD, qc),
                               lambda bi, h, i: (bi * (s // tq) + i, h, 0, 0)),
                  pl.BlockSpec((None, s, MLA_QK_PAD), lambda bi, h, i: (bi, 0, h)),
                  pl.BlockSpec((s // qc, None, MLA_V, qc), lambda bi, h, i: (bi, h, 0, 0))],
        out_specs=pl.BlockSpec((None, tq, MLA_V), lambda bi, h, i: (bi, i, h)),
        out_shape=jax.ShapeDtypeStruct((b, s, MLA_HEADS * MLA_V), BF16),
        scratch_shapes=[pltpu.VMEM((n_sub, 1, qc), F32), pltpu.VMEM((n_sub, 1, qc), F32),
                        pltpu.VMEM((n_sub, MLA_V, qc), F32),
                        pltpu.VMEM((n_sub, MLA_QK_PAD, qc), BF16),
                        pltpu.VMEM((1, qc), F32)],
        compiler_params=_params(3),
        name="attn",
    )(q_t, k.reshape(b, s, MLA_HEADS * MLA_QK_PAD), v_t)


def _softplus(x):
    return jnp.maximum(x, 0.0) + jnp.log1p(jnp.exp(-jnp.abs(x)))


def _hyb_body(h_ref, pre_g_ref, w_in_ref, ln_g_ref, ln_b_ref, ws_ref, bs_ref,
              conv_w_ref, conv_b_ref, dt_bias_ref, a_log_ref, dskip_ref, ng_ref,
              w_out_ref, post_g_ref, o_ref,
              ext_ref, state_ref, y_ref, xs_ref, b_ref, c_ref, z_ref, dt_ref, da_ref, *, tm, n_split):
    gr = tm // n_split
    nc = gr // CHUNK
    t_idx = pl.program_id(1)

    @pl.when(t_idx == 0)
    def _():
        ext_ref[0:CONV_HALO, :] = jnp.zeros((CONV_HALO, SSD_CONV_CH), F32)
        state_ref[...] = jnp.zeros(state_ref.shape, F32)

    row = lax.broadcasted_iota(jnp.int32, (CHUNK, CHUNK), 0)
    col = lax.broadcasted_iota(jnp.int32, (CHUNK, CHUNK), 1)
    causal = col <= row

    def in_proj(r0):
        rows = slice(r0, r0 + gr)
        h = h_ref[rows, :]
        hn = _rms(h, pre_g_ref[...]).astype(BF16)
        u_pre = _dot(hn, w_in_ref[:, 0:GM_WIDTH])
        v_pre = _dot(hn, w_in_ref[:, GM_WIDTH:2 * GM_WIDTH])
        o = 2 * GM_WIDTH
        z_ref[rows, :] = _dot(hn, w_in_ref[:, o:o + SSD_INNER])
        o += SSD_INNER
        ext_ref[CONV_HALO + r0:CONV_HALO + r0 + gr, :] = _dot(hn, w_in_ref[:, o:o + SSD_CONV_CH])
        o += SSD_CONV_CH
        dt_pre = _dot(hn, w_in_ref[:, o:o + LANES])
        return h, u_pre, v_pre, dt_pre

    def gmlp(r0, u_pre, v_pre):
        u = jax.nn.gelu(u_pre)
        v = jax.nn.gelu(v_pre)
        for hd in range(GM_HEADS):
            sl = slice(hd * GM_HEAD_DIM, (hd + 1) * GM_HEAD_DIM)
            vh = v[:, sl]
            mu = jnp.mean(vh, axis=-1, keepdims=True)
            xc = vh - mu
            var = jnp.mean(xc * xc, axis=-1, keepdims=True)
            vn = (xc * lax.rsqrt(var + LN_EPS) * ln_g_ref[:, sl] + ln_b_ref[:, sl]).astype(BF16)
            rhs = jnp.concatenate([vn[c * CHUNK:(c + 1) * CHUNK] for c in range(nc)], axis=1)
            w = jnp.where(causal, ws_ref[hd], 0.0).astype(BF16)
            mixed = _dot(w, rhs)
            bias = bs_ref[:, hd:hd + 1]
            for c in range(nc):
                y_ref[r0 + c * CHUNK:r0 + (c + 1) * CHUNK, sl] = (
                    u[c * CHUNK:(c + 1) * CHUNK, sl] * (mixed[:, c * CHUNK:(c + 1) * CHUNK] + bias)
                ).astype(BF16)

    def ssd_front(r0, dt_pre):
        rows = slice(r0, r0 + gr)
        dt = _softplus(dt_pre + dt_bias_ref[...])
        dt_ref[rows, :] = dt
        da_ref[rows, :] = dt * (-jnp.exp(a_log_ref[...]))
        conv = conv_b_ref[...]
        for kk in range(SSD_CONV):
            lo = r0 + CONV_HALO - (SSD_CONV - 1) + kk
            conv = conv + conv_w_ref[kk:kk + 1, :] * ext_ref[lo:lo + gr, :]
        xbc = conv * jax.nn.sigmoid(conv)
        xs_ref[rows, :] = xbc[:, :SSD_INNER]
        b_ref[rows, :] = xbc[:, SSD_INNER:SSD_INNER + SSD_GROUPS * SSD_STATE]
        c_ref[rows, :] = xbc[:, SSD_INNER + SSD_GROUPS * SSD_STATE:]

    tril = jnp.where(causal, 1.0, 0.0).astype(F32)
    lane = lax.broadcasted_iota(jnp.int32, (CHUNK, LANES), 1)
    first_head = lane < SSD_HEAD_DIM

    def chunk_body(c, carry):
        rows = pl.ds(c * CHUNK, CHUNK)
        dt_c = dt_ref[rows, :]
        acs = jnp.dot(tril, da_ref[rows, :], precision=lax.Precision.HIGHEST,
                      preferred_element_type=F32)
        acs_t = acs.T
        bc = b_ref[rows, :]
        cc = c_ref[rows, :]
        for g in range(SSD_GROUPS):
            gs = slice(g * SSD_STATE, (g + 1) * SSD_STATE)
            bg_t = bc[:, gs].T.astype(BF16)
            cg = cc[:, gs].astype(BF16)
            cb = _dot(cg, bg_t)
            ys = []
            for pj in range(SSD_PAIRS // SSD_GROUPS):
                j = g * (SSD_PAIRS // SSD_GROUPS) + pj
                ha, hb = 2 * j, 2 * j + 1
                sl = slice(j * LANES, (j + 1) * LANES)
                acol_a = jnp.broadcast_to(acs[:, ha:ha + 1], (CHUNK, LANES))
                acol_b = jnp.broadcast_to(acs[:, hb:hb + 1], (CHUNK, LANES))
                arow_a = jnp.broadcast_to(acs_t[ha:ha + 1, :], (CHUNK, CHUNK))
                arow_b = jnp.broadcast_to(acs_t[hb:hb + 1, :], (CHUNK, CHUNK))
                dtw = jnp.where(first_head, jnp.broadcast_to(dt_c[:, ha:ha + 1], (CHUNK, LANES)),
                                jnp.broadcast_to(dt_c[:, hb:hb + 1], (CHUNK, LANES)))
                acw = jnp.where(first_head, acol_a, acol_b)
                xs_p = xs_ref[rows, sl]
                xd = xs_p * dtw
                xd_bf = xd.astype(BF16)
                m_a = (cb * jnp.exp(jnp.where(causal, acol_a - arow_a, -jnp.inf))).astype(BF16)
                m_b = (cb * jnp.exp(jnp.where(causal, acol_b - arow_b, -jnp.inf))).astype(BF16)
                y_diag = jnp.where(first_head, _dot(m_a, xd_bf), _dot(m_b, xd_bf))
                acw_last = acw[CHUNK - 1:CHUNK, :]
                xw = (xd * jnp.exp(acw_last - acw)).astype(BF16)
                st_new = _dot(bg_t, xw)
                prev = state_ref[:, sl]
                y_off = _dot(cg, prev.astype(BF16)) * jnp.exp(acw)
                state_ref[:, sl] = prev * jnp.exp(acw_last) + st_new
                zz = z_ref[rows, sl]
                y = y_diag + y_off + dskip_ref[:, sl] * xs_p
                ys.append(y * (zz * jax.nn.sigmoid(zz)))
            yg = jnp.concatenate(ys, axis=1)
            gw = SSD_INNER // SSD_GROUPS
            yn = _rms(yg, ng_ref[:, g * gw:(g + 1) * gw])
            y_ref[rows, GM_WIDTH + g * gw:GM_WIDTH + (g + 1) * gw] = yn.astype(BF16)
        return carry

    def out_proj(r0, h):
        rows = slice(r0, r0 + gr)
        mixed = _dot(y_ref[rows, :], w_out_ref[...])
        o_ref[rows, :] = h + _rms(mixed, post_g_ref[...])

    starts = [g * gr for g in range(n_split)]
    staged = [in_proj(r0) for r0 in starts]
    for r0, (_, u_pre, v_pre, dt_pre) in zip(starts, staged):
        gmlp(r0, u_pre, v_pre)
        ssd_front(r0, dt_pre)
    ext_ref[0:CONV_HALO, :] = ext_ref[tm:tm + CONV_HALO, :]
    for r0, (h, _, _, _) in zip(starts, staged):
        for c in range(nc):
            chunk_body(r0 // CHUNK + c, 0)
        out_proj(r0, h)


def _hyb_weights(w_in, ln_g, ln_b, w_s, b_s, conv_w, conv_b, dt_bias, a_log, d_skip, norm_g, w_out):
    n = w_in.shape[0]
    pad = (-w_in.shape[2]) % LANES
    w_in_k = jnp.pad(w_in, ((0, 0), (0, 0), (0, pad))).astype(BF16)
    row = lambda a: a.reshape(n, 1, -1)
    pad_h = lambda a: jnp.pad(row(a), ((0, 0), (0, 0), (0, LANES - SSD_HEADS)))
    return (w_in_k, row(ln_g), row(ln_b), w_s, jnp.swapaxes(b_s, 1, 2), conv_w, row(conv_b),
            pad_h(dt_bias), pad_h(a_log), row(jnp.repeat(d_skip, SSD_HEAD_DIM, axis=1)), row(norm_g),
            w_out.astype(BF16))


def _hybrid(h, layer, j, pre_g, weights, post_g):
    b, s, d = h.shape
    tm = min(TM_HYB, s)
    row = pl.BlockSpec((None, tm, d), lambda bi, i: (bi, i, 0))
    return pl.pallas_call(
        functools.partial(_hyb_body, tm=tm, n_split=math.gcd(HYB_SPLIT, tm // CHUNK)),
        grid=(b, s // tm),
        in_specs=([row, _layer_spec(pre_g, layer)] + [_layer_spec(w, j) for w in weights]
                  + [_layer_spec(post_g, layer)]),
        out_specs=row,
        out_shape=jax.ShapeDtypeStruct((b, s, d), F32),
        scratch_shapes=[
            pltpu.VMEM((tm + CONV_HALO, SSD_CONV_CH), F32),
            pltpu.VMEM((SSD_STATE, SSD_INNER), F32),
            pltpu.VMEM((tm, GM_WIDTH + SSD_INNER), BF16),
            pltpu.VMEM((tm, SSD_INNER), F32),
            pltpu.VMEM((tm, SSD_GROUPS * SSD_STATE), F32),
            pltpu.VMEM((tm, SSD_GROUPS * SSD_STATE), F32),
            pltpu.VMEM((tm, SSD_INNER), F32),
            pltpu.VMEM((tm, LANES), F32),
            pltpu.VMEM((tm, LANES), F32),
        ],
        compiler_params=_params(2),
        name="hybrid",
    )(h, pre_g, *weights, post_g)


def kernel(x, p, positions, ffn1_pre_g, ffn1_w_in, ffn1_w_down, ffn1_post_g, mix_pre_g, mix_post_g, ffn2_pre_g, ffn2_w_in, ffn2_w_down, ffn2_post_g, ple_pre_g, ple_w_gate, ple_w_proj, ple_post_g, hyb_w_in, gm_ln_g, gm_ln_b, gm_w_s, gm_b_s, ssd_conv_w, ssd_conv_b, ssd_dt_bias, ssd_a_log, ssd_d, ssd_norm_g, hyb_w_out, mla_w_in, mla_q_norm_g, mla_kv_norm_g, mla_w_uq, mla_w_ukv, mla_w_out):
    b, s, d = x.shape
    t = b * s
    depth = ffn1_w_in.shape[0]
    g3 = lambda a: a.reshape(a.shape[0], 1, -1)
    bf = lambda a: a.astype(BF16)
    ffn1 = (g3(ffn1_pre_g), bf(ffn1_w_in), bf(ffn1_w_down), g3(ffn1_post_g))
    ffn2 = (g3(ffn2_pre_g), bf(ffn2_w_in), bf(ffn2_w_down), g3(ffn2_post_g))
    ple = (p.reshape(depth, t, -1), g3(ple_pre_g), bf(ple_w_gate), bf(ple_w_proj), g3(ple_post_g))
    mix_pre, mix_post = g3(mix_pre_g), g3(mix_post_g)
    hyb_w = _hyb_weights(hyb_w_in, gm_ln_g, gm_ln_b, gm_w_s, gm_b_s, ssd_conv_w, ssd_conv_b,
                         ssd_dt_bias, ssd_a_log, ssd_d, ssd_norm_g, hyb_w_out)
    mla_w = _mla_weights(mla_w_in, mla_w_uq, mla_w_ukv)
    mla_w_o = bf(mla_w_out)
    qc = min(QC_ATTN, s)
    h = x.reshape(t, d)
    for i in range(depth):
        j = i // 2
        h = _ffn(h, i, *ffn1)
        if i % 2 == 0:
            h = _hybrid(h.reshape(b, s, d), i, j, mix_pre, hyb_w, mix_post).reshape(t, d)
            h = _ffn(h, i, *ffn2, ple=ple)
        else:
            q_t, k, v_t = _mla_proj(h, positions, i, j, mix_pre, mla_w, g3(mla_q_norm_g),
                                    g3(mla_kv_norm_g), qc)
            o = _attn(q_t, k, v_t, b, s, qc)
            h = _ffn(h, i, *ffn2, proj=(o.reshape(t, -1), (j, mla_w_o), (i, mix_post)), ple=ple)
    return h.reshape(b, s, d)
```

```python
import functools
import math

import jax
import jax.numpy as jnp
from jax import lax
from jax.experimental import pallas as pl
from jax.experimental.pallas import tpu as pltpu

F32 = jnp.float32
BF16 = jnp.bfloat16

NORM_EPS = 1e-6
LN_EPS = 1e-5

LANES = 128
CHUNK = 128

GM_HEADS = 8
GM_HEAD_DIM = 128
GM_WIDTH = GM_HEADS * GM_HEAD_DIM

SSD_HEAD_DIM = 64
SSD_HEADS = 16
SSD_INNER = SSD_HEADS * SSD_HEAD_DIM
SSD_GROUPS = 2
SSD_STATE = 128
SSD_CONV = 4
SSD_CONV_CH = SSD_INNER + 2 * SSD_GROUPS * SSD_STATE
SSD_PAIRS = SSD_HEADS // 2
CONV_HALO = 8

MLA_HEADS = 16
MLA_NOPE = 128
MLA_ROPE = 64
MLA_V = 128
MLA_Q_LORA = 256
MLA_KV_LORA = 128
MLA_QK = MLA_NOPE + MLA_ROPE
MLA_QK_PAD = 256
ROPE_BASE = 10000.0
OFFSET_ROW = MLA_QK
FAST_LIMIT = 60.0

FFN_CHUNK = 256
FFN_GROUP = 256
HYB_SPLIT = 2
VMEM_LIMIT = 56 * 1024 * 1024

TM_FFN = 512
TM_HYB = 512
QC_ATTN = 512
TQ_ATTN = 4096
TK_ATTN = 1024
KB_UNROLL = 2


def _rms(x, g):
    return x * lax.rsqrt(jnp.mean(x * x, axis=-1, keepdims=True) + NORM_EPS) * g


def _dot(a, b):
    return jnp.dot(a, b, preferred_element_type=F32)


def _const_spec(shape):
    nd = len(shape)
    return pl.BlockSpec(shape, lambda *_: (0,) * nd, pipeline_mode=pl.Buffered(1))


def _layer_spec(stacked, layer):
    zeros = (0,) * (stacked.ndim - 1)
    return pl.BlockSpec((None,) + stacked.shape[1:], lambda *_: (layer,) + zeros,
                        pipeline_mode=pl.Buffered(1))


def _params(n_axes):
    return pltpu.CompilerParams(
        dimension_semantics=("arbitrary",) * n_axes, vmem_limit_bytes=VMEM_LIMIT)


def _ffn_body(*refs, d_ff, with_proj, with_ple):
    refs = list(refs)
    h_ref = refs.pop(0)
    tm = h_ref.shape[0]
    grp = min(FFN_GROUP, tm)
    parts = [slice(r0, r0 + grp) for r0 in range(0, tm, grp)]
    h = [h_ref[p, :] for p in parts]
    if with_proj:
        o_in_ref, w_o_ref, mix_g_ref = refs[:3]
        del refs[:3]
        mixed = [_dot(o_in_ref[p, :], w_o_ref[...]) for p in parts]
        h = [hh + _rms(mm, mix_g_ref[...]) for hh, mm in zip(h, mixed)]
    pre_g_ref, w_in_ref, w_down_ref, post_g_ref = refs[:4]
    del refs[:4]
    if with_ple:
        p_ref, ple_pre_g_ref, wg_ref, wp_ref, ple_post_g_ref = refs[:5]
        del refs[:5]
    o_ref, act_ref = refs

    xn = [_rms(hh, pre_g_ref[...]).astype(BF16) for hh in h]
    for c in range(d_ff // FFN_CHUNK):
        lo = c * FFN_CHUNK
        for p, x in zip(parts, xn):
            g = _dot(x, w_in_ref[:, lo:lo + FFN_CHUNK])
            u = _dot(x, w_in_ref[:, d_ff + lo:d_ff + lo + FFN_CHUNK])
            act_ref[p, lo:lo + FFN_CHUNK] = (g * jax.nn.sigmoid(g) * u).astype(BF16)
    f = [_dot(act_ref[p, :], w_down_ref[...]) for p in parts]
    if with_ple:
        proj = [_dot(p_ref[p, :].astype(BF16), wp_ref[...]) for p in parts]
    h = [hh + 0.5 * _rms(ff, post_g_ref[...]) for hh, ff in zip(h, f)]
    if with_ple:
        xn = [_rms(hh, ple_pre_g_ref[...]).astype(BF16) for hh in h]
        gate = [jax.nn.sigmoid(_dot(x, wg_ref[...])) for x in xn]
        h = [hh + _rms(gg * pp, ple_post_g_ref[...]) for hh, gg, pp in zip(h, gate, proj)]
    for p, hh in zip(parts, h):
        o_ref[p, :] = hh


def _ffn(h, layer, pre_g, w_in, w_down, post_g, proj=None, ple=None):
    t, d = h.shape
    d_ff = w_down.shape[1]
    tm = min(TM_FFN if (proj is not None or ple is not None) else 2 * TM_FFN, t)
    row = pl.BlockSpec((tm, d), lambda i: (i, 0))
    args, specs = [h], [row]
    if proj is not None:
        o_in, (lw, w_o), (lg, mix_g) = proj
        args += [o_in, w_o, mix_g]
        specs += [pl.BlockSpec((tm, o_in.shape[1]), lambda i: (i, 0)), _layer_spec(w_o, lw),
                  _layer_spec(mix_g, lg)]
    stacked = [pre_g, w_in, w_down, post_g]
    args += stacked
    specs += [_layer_spec(a, layer) for a in stacked]
    if ple is not None:
        args += list(ple)
        specs += [pl.BlockSpec((None, tm, ple[0].shape[2]), lambda i: (layer, i, 0))]
        specs += [_layer_spec(a, layer) for a in ple[1:]]
    return pl.pallas_call(
        functools.partial(_ffn_body, d_ff=d_ff, with_proj=proj is not None, with_ple=ple is not None),
        grid=(t // tm,),
        in_specs=specs,
        out_specs=row,
        out_shape=jax.ShapeDtypeStruct((t, d), F32),
        scratch_shapes=[pltpu.VMEM((tm, d_ff), BF16)],
        compiler_params=_params(1),
        name="ffn",
    )(*args)


def _mla_proj_body(h_ref, pos_ref, inv_ref, pre_g_ref, w_in_ref, qg_ref, kvg_ref,
                   wq_t_ref, wk_ref, wv_t_ref, q_t_ref, k_ref, v_t_ref, *, q_scale):
    tm = h_ref.shape[0]
    ang = inv_ref[...] * pos_ref[...].astype(F32)
    cos32 = jnp.cos(ang)
    sin32 = jnp.sin(ang)
    pad = jnp.zeros((LANES - MLA_ROPE, tm), F32)
    cos_t = jnp.concatenate([cos32, cos32, pad], axis=0)
    sin_t = jnp.concatenate([-sin32, sin32, pad], axis=0)
    cos_n = cos_t.T
    sin_n = sin_t.T
    hn = _rms(h_ref[...], pre_g_ref[...]).astype(BF16)
    c = _dot(hn, w_in_ref[...])
    cq = _rms(c[:, :MLA_Q_LORA], qg_ref[...])
    ckv = _rms(c[:, MLA_Q_LORA:MLA_Q_LORA + MLA_KV_LORA], kvg_ref[...])
    o = MLA_Q_LORA + MLA_KV_LORA
    one = jnp.where(lax.broadcasted_iota(jnp.int32, (1, LANES), 1) == OFFSET_ROW - MLA_NOPE, 1.0, 0.0)
    kr = (c[:, o:o + LANES] * cos_n + c[:, o + LANES:o + 2 * LANES] * sin_n + one).astype(BF16)
    cq_t = cq.T.astype(BF16)
    ckv_t = ckv.T.astype(BF16)
    kn = _dot(ckv.astype(BF16), wk_ref[...])
    v_t = _dot(wv_t_ref[...], ckv_t)
    for hd in range(MLA_HEADS):
        k_ref[:, hd * MLA_QK_PAD:hd * MLA_QK_PAD + MLA_NOPE] = (
            kn[:, hd * MLA_NOPE:(hd + 1) * MLA_NOPE].astype(BF16))
        k_ref[:, hd * MLA_QK_PAD + MLA_NOPE:(hd + 1) * MLA_QK_PAD] = kr
        v_t_ref[hd] = v_t[hd * MLA_V:(hd + 1) * MLA_V, :].astype(BF16)
        qh = _dot(wq_t_ref[hd], cq_t)
        qr = qh[MLA_NOPE:MLA_NOPE + LANES] * cos_t + qh[MLA_NOPE + LANES:] * sin_t
        q_t_ref[hd, 0:MLA_NOPE, :] = (qh[:MLA_NOPE] * q_scale).astype(BF16)
        q_t_ref[hd, MLA_NOPE:MLA_QK_PAD, :] = (qr * q_scale).astype(BF16)


def _mla_weights(w_in, w_uq, w_ukv):
    n = w_in.shape[0]
    half = MLA_ROPE // 2

    def rope_slabs(w):
        x1, x2 = w[..., :half], w[..., half:]
        z = jnp.zeros(w.shape[:-1] + (LANES - MLA_ROPE,), w.dtype)
        return jnp.concatenate([x1, x2, z, x2, x1, z], axis=-1)

    o = MLA_Q_LORA + MLA_KV_LORA
    w_in_k = jnp.concatenate([w_in[..., :o], rope_slabs(w_in[..., o:])], axis=-1).astype(BF16)
    wq = w_uq.reshape(n, MLA_Q_LORA, MLA_HEADS, MLA_QK)
    wq = jnp.concatenate([wq[..., :MLA_NOPE], rope_slabs(wq[..., MLA_NOPE:])], axis=-1)
    wq_t = jnp.transpose(wq, (0, 2, 3, 1)).astype(BF16)
    wkv = w_ukv.reshape(n, MLA_KV_LORA, MLA_HEADS, MLA_NOPE + MLA_V)
    wk = wkv[..., :MLA_NOPE].reshape(n, MLA_KV_LORA, MLA_HEADS * MLA_NOPE).astype(BF16)
    wv = wkv[..., MLA_NOPE:].reshape(n, MLA_KV_LORA, MLA_HEADS * MLA_V)
    wv_t = jnp.swapaxes(wv, 1, 2).astype(BF16)
    return w_in_k, wq_t, wk, wv_t


def _mla_proj(h, pos, layer, j, pre_g, weights, q_norm_g, kv_norm_g, qc):
    t, d = h.shape
    half = MLA_ROPE // 2
    inv = 1.0 / (ROPE_BASE ** (jnp.arange(0, MLA_ROPE, 2, dtype=F32) / MLA_ROPE))
    q_scale = (MLA_QK ** -0.5) * math.log2(math.e)
    w_in_k, wq_t, wk, wv_t = weights
    n = t // qc
    return pl.pallas_call(
        functools.partial(_mla_proj_body, q_scale=q_scale),
        grid=(n,),
        in_specs=[pl.BlockSpec((qc, d), lambda i: (i, 0)), pl.BlockSpec((1, qc), lambda i: (0, i)),
                  _const_spec((half, 1)), _layer_spec(pre_g, layer), _layer_spec(w_in_k, j),
                  _layer_spec(q_norm_g, j), _layer_spec(kv_norm_g, j),
                  _layer_spec(wq_t, j), _layer_spec(wk, j), _layer_spec(wv_t, j)],
        out_specs=[pl.BlockSpec((None, MLA_HEADS, MLA_QK_PAD, qc), lambda i: (i, 0, 0, 0)),
                   pl.BlockSpec((qc, MLA_HEADS * MLA_QK_PAD), lambda i: (i, 0)),
                   pl.BlockSpec((None, MLA_HEADS, MLA_V, qc), lambda i: (i, 0, 0, 0))],
        out_shape=[jax.ShapeDtypeStruct((n, MLA_HEADS, MLA_QK_PAD, qc), BF16),
                   jax.ShapeDtypeStruct((t, MLA_HEADS * MLA_QK_PAD), BF16),
                   jax.ShapeDtypeStruct((n, MLA_HEADS, MLA_V, qc), BF16)],
        compiler_params=_params(1),
        name="mla_proj",
    )(h, pos.reshape(1, t), inv.reshape(half, 1), pre_g, w_in_k, q_norm_g, kv_norm_g, wq_t, wk, wv_t)


def _qk_t(k, q_t, mask):
    s_t = _dot(k, q_t)
    if mask is not None:
        s_t = jnp.where(mask, s_t, -jnp.inf)
    return s_t


def _softmax_pv_t(s_t, v_t, m_ref, l_ref, acc_ref, r):
    m_prev = m_ref[r]
    m_new = jnp.maximum(m_prev, jnp.max(s_t, axis=0, keepdims=True))
    alpha = jnp.exp2(m_prev - m_new)
    p_t = jnp.exp2(s_t - m_new)
    l_ref[r] = alpha * l_ref[r] + jnp.sum(p_t, axis=0, keepdims=True)
    acc_ref[r] = _dot(v_t, p_t.astype(BF16)) + alpha * acc_ref[r]
    m_ref[r] = m_new


def _lagged_pv_t(s_t, v_t, m_ref, l_ref, acc_ref, qa_ref, excess_ref, r):
    p_t = jnp.exp2(s_t)
    top = jnp.max(s_t, axis=0, keepdims=True)
    l_new = l_ref[r] + jnp.sum(p_t, axis=0, keepdims=True)
    acc_new = _dot(v_t, p_t.astype(BF16)) + acc_ref[r]
    m_old = m_ref[r]
    m_new = (m_old + jnp.maximum(top, 0.0)).astype(BF16).astype(F32)
    alpha = jnp.exp2(m_old - m_new)
    l_ref[r] = l_new * alpha
    acc_ref[r] = acc_new * alpha
    m_ref[r] = m_new
    qa_ref[r, OFFSET_ROW:OFFSET_ROW + 1, :] = (-m_new).astype(BF16)
    excess_ref[...] = jnp.maximum(excess_ref[...], top)


def _attn_body(q_t_ref, k_ref, v_t_ref, o_ref, m_ref, l_ref, acc_ref, qa_ref, excess_ref,
               *, tq, tk, qc, unroll):
    i = pl.program_id(2)
    n_sub = tq // qc
    kc = tk // qc
    n_blocks = i * (tq // tk) // unroll
    d0 = pl.multiple_of(i * tq, tq)
    krow = lax.broadcasted_iota(jnp.int32, (qc, qc), 0)
    qcol = lax.broadcasted_iota(jnp.int32, (qc, qc), 1)
    tri = krow <= qcol

    def reset():
        m_ref[...] = jnp.full(m_ref.shape, -jnp.inf, F32)
        l_ref[...] = jnp.zeros(l_ref.shape, F32)
        acc_ref[...] = jnp.zeros(acc_ref.shape, F32)

    def exact(s_t, v_t, r):
        _softmax_pv_t(s_t, v_t, m_ref, l_ref, acc_ref, r)

    def lagged(s_t, v_t, r):
        _lagged_pv_t(s_t, v_t, m_ref, l_ref, acc_ref, qa_ref, excess_ref, r)

    def pipelined(items, update):
        s_t = items[0][0]()
        for n, (_, v_t, r) in enumerate(items):
            more = n + 1 < len(items)
            ahead = more and items[n + 1][2] != r
            s_next = items[n + 1][0]() if ahead else None
            update(s_t, v_t, r)
            s_t = items[n + 1][0]() if more and not ahead else s_next

    def tile_keys(c0, c1):
        k = k_ref[pl.ds(pl.multiple_of(d0 + c0 * qc, qc), (c1 - c0) * qc), :]
        v_t = jnp.concatenate([v_t_ref[i * n_sub + c] for c in range(c0, c1)], axis=1)
        return k, v_t

    def block_keys(jb):
        k = k_ref[pl.ds(pl.multiple_of(jb * tk, tk), tk), :]
        v_t = jnp.concatenate([v_t_ref[jb * kc + c] for c in range(kc)], axis=1)
        return k, v_t

    def block_items(j, q_ref):
        items = []
        for u in range(unroll):
            k, v_t = block_keys(j * unroll + u)
            items += [(lambda k=k, r=r: _qk_t(k, q_ref[r], None), v_t, r) for r in range(n_sub)]
        return items

    def finish():
        for r in range(n_sub):
            o_t = acc_ref[r] / l_ref[r]
            o_ref[r * qc:(r + 1) * qc, :] = o_t.T.astype(o_ref.dtype)

    reset()
    excess_ref[...] = jnp.zeros(excess_ref.shape, F32)

    def own_scores(k, r):
        s_t = _qk_t(k, q_t_ref[r], None)
        m0 = s_t[0:1, :].astype(BF16).astype(F32)
        m_ref[r] = m0
        qa_ref[r] = q_t_ref[r]
        return jnp.where(tri, s_t - m0, -jnp.inf)

    diag = []
    for r in reversed(range(n_sub)):
        k, v_t = tile_keys(r, r + 1)
        diag.append((lambda k=k, r=r: own_scores(k, r), v_t, r))
    for r in reversed(range(1, n_sub)):
        for c0 in range(0, r, kc):
            k, v_t = tile_keys(c0, min(c0 + kc, r))
            diag.append((lambda k=k, r=r: _qk_t(k, qa_ref[r], None), v_t, r))
    pipelined(diag, lagged)


    def lagged_body(j, carry):
        pipelined(block_items(j, qa_ref), lagged)
        return carry

    lax.fori_loop(0, n_blocks, lagged_body, 0)
    finish()

    @pl.when(jnp.max(excess_ref[...]) > FAST_LIMIT)
    def _():
        reset()

        def exact_body(j, carry):
            pipelined(block_items(j, q_t_ref), exact)
            return carry

        lax.fori_loop(0, n_blocks, exact_body, 0)
        diag = []
        for r in range(n_sub):
            k, v_t = tile_keys(0, r + 1)
            mask = tri if r == 0 else jnp.concatenate([jnp.ones((r * qc, qc), jnp.bool_), tri], axis=0)
            diag.append((lambda k=k, r=r, mask=mask: _qk_t(k, q_t_ref[r], mask), v_t, r))
        pipelined(diag, exact)
        finish()


def _attn(q_t, k, v_t, b, s, qc):
    tq = min(TQ_ATTN, s)
    tk = min(TK_ATTN, tq)
    n_sub = tq // qc
    unroll = math.gcd(KB_UNROLL, tq // tk)
    return pl.pallas_call(
        functools.partial(_attn_body, tq=tq, tk=tk, qc=qc, unroll=unroll),
        grid=(b, MLA_HEADS, s // tq),
        in_specs=[pl.BlockSpec((n_sub, None, MLA_QK_PAD, qc),
                               lambda bi, h, i: (bi * (s // tq) + i, h, 0, 0)),
                  pl.BlockSpec((None, s, MLA_QK_PAD), lambda bi, h, i: (bi, 0, h)),
                  pl.BlockSpec((s // qc, None, MLA_V, qc), lambda bi, h, i: (bi, h, 0, 0))],
        out_specs=pl.BlockSpec((None, tq, MLA_V), lambda bi, h, i: (bi, i, h)),
        out_shape=jax.ShapeDtypeStruct((b, s, MLA_HEADS * MLA_V), BF16),
        scratch_shapes=[pltpu.VMEM((n_sub, 1, qc), F32), pltpu.VMEM((n_sub, 1, qc), F32),
                        pltpu.VMEM((n_sub, MLA_V, qc), F32),
                        pltpu.VMEM((n_sub, MLA_QK_PAD, qc), BF16),
                        pltpu.VMEM((1, qc), F32)],
        compiler_params=_params(3),
        name="attn",
    )(q_t, k.reshape(b, s, MLA_HEADS * MLA_QK_PAD), v_t)


def _softplus(x):
    return jnp.maximum(x, 0.0) + jnp.log1p(jnp.exp(-jnp.abs(x)))


def _hyb_body(h_ref, pre_g_ref, w_in_ref, ln_g_ref, ln_b_ref, ws_ref, bs_ref,
              conv_w_ref, conv_b_ref, dt_bias_ref, a_log_ref, dskip_ref, ng_ref,
              w_out_ref, post_g_ref, o_ref,
              ext_ref, state_ref, y_ref, xs_ref, b_ref, c_ref, z_ref, dt_ref, da_ref, *, tm, n_split):
    gr = tm // n_split
    nc = gr // CHUNK
    t_idx = pl.program_id(1)

    @pl.when(t_idx == 0)
    def _():
        ext_ref[0:CONV_HALO, :] = jnp.zeros((CONV_HALO, SSD_CONV_CH), F32)
        state_ref[...] = jnp.zeros(state_ref.shape, F32)

    row = lax.broadcasted_iota(jnp.int32, (CHUNK, CHUNK), 0)
    col = lax.broadcasted_iota(jnp.int32, (CHUNK, CHUNK), 1)
    causal = col <= row

    def in_proj(r0):
        rows = slice(r0, r0 + gr)
        h = h_ref[rows, :]
        hn = _rms(h, pre_g_ref[...]).astype(BF16)
        u_pre = _dot(hn, w_in_ref[:, 0:GM_WIDTH])
        v_pre = _dot(hn, w_in_ref[:, GM_WIDTH:2 * GM_WIDTH])
        o = 2 * GM_WIDTH
        z_ref[rows, :] = _dot(hn, w_in_ref[:, o:o + SSD_INNER])
        o += SSD_INNER
        ext_ref[CONV_HALO + r0:CONV_HALO + r0 + gr, :] = _dot(hn, w_in_ref[:, o:o + SSD_CONV_CH])
        o += SSD_CONV_CH
        dt_pre = _dot(hn, w_in_ref[:, o:o + LANES])
        return h, u_pre, v_pre, dt_pre

    def gmlp(r0, u_pre, v_pre):
        u = jax.nn.gelu(u_pre)
        v = jax.nn.gelu(v_pre)
        for hd in range(GM_HEADS):
            sl = slice(hd * GM_HEAD_DIM, (hd + 1) * GM_HEAD_DIM)
            vh = v[:, sl]
            mu = jnp.mean(vh, axis=-1, keepdims=True)
            xc = vh - mu
            var = jnp.mean(xc * xc, axis=-1, keepdims=True)
            vn = (xc * lax.rsqrt(var + LN_EPS) * ln_g_ref[:, sl] + ln_b_ref[:, sl]).astype(BF16)
            rhs = jnp.concatenate([vn[c * CHUNK:(c + 1) * CHUNK] for c in range(nc)], axis=1)
            w = jnp.where(causal, ws_ref[hd], 0.0).astype(BF16)
            mixed = _dot(w, rhs)
            bias = bs_ref[:, hd:hd + 1]
            for c in range(nc):
                y_ref[r0 + c * CHUNK:r0 + (c + 1) * CHUNK, sl] = (
                    u[c * CHUNK:(c + 1) * CHUNK, sl] * (mixed[:, c * CHUNK:(c + 1) * CHUNK] + bias)
                ).astype(BF16)

    def ssd_front(r0, dt_pre):
        rows = slice(r0, r0 + gr)
        dt = _softplus(dt_pre + dt_bias_ref[...])
        dt_ref[rows, :] = dt
        da_ref[rows, :] = dt * (-jnp.exp(a_log_ref[...]))
        conv = conv_b_ref[...]
        for kk in range(SSD_CONV):
            lo = r0 + CONV_HALO - (SSD_CONV - 1) + kk
            conv = conv + conv_w_ref[kk:kk + 1, :] * ext_ref[lo:lo + gr, :]
        xbc = conv * jax.nn.sigmoid(conv)
        xs_ref[rows, :] = xbc[:, :SSD_INNER]
        b_ref[rows, :] = xbc[:, SSD_INNER:SSD_INNER + SSD_GROUPS * SSD_STATE]
        c_ref[rows, :] = xbc[:, SSD_INNER + SSD_GROUPS * SSD_STATE:]

    tril = jnp.where(causal, 1.0, 0.0).astype(F32)
    lane = lax.broadcasted_iota(jnp.int32, (CHUNK, LANES), 1)
    first_head = lane < SSD_HEAD_DIM

    def chunk_body(c, carry):
        rows = pl.ds(c * CHUNK, CHUNK)
        dt_c = dt_ref[rows, :]
        acs = jnp.dot(tril, da_ref[rows, :], precision=lax.Precision.HIGHEST,
                      preferred_element_type=F32)
        acs_t = acs.T
        bc = b_ref[rows, :]
        cc = c_ref[rows, :]
        for g in range(SSD_GROUPS):
            gs = slice(g * SSD_STATE, (g + 1) * SSD_STATE)
            bg_t = bc[:, gs].T.astype(BF16)
            cg = cc[:, gs].astype(BF16)
            cb = _dot(cg, bg_t)
            ys = []
            for pj in range(SSD_PAIRS // SSD_GROUPS):
                j = g * (SSD_PAIRS // SSD_GROUPS) + pj
                ha, hb = 2 * j, 2 * j + 1
                sl = slice(j * LANES, (j + 1) * LANES)
                acol_a = jnp.broadcast_to(acs[:, ha:ha + 1], (CHUNK, LANES))
                acol_b = jnp.broadcast_to(acs[:, hb:hb + 1], (CHUNK, LANES))
                arow_a = jnp.broadcast_to(acs_t[ha:ha + 1, :], (CHUNK, CHUNK))
                arow_b = jnp.broadcast_to(acs_t[hb:hb + 1, :], (CHUNK, CHUNK))
                dtw = jnp.where(first_head, jnp.broadcast_to(dt_c[:, ha:ha + 1], (CHUNK, LANES)),
                                jnp.broadcast_to(dt_c[:, hb:hb + 1], (CHUNK, LANES)))
                acw = jnp.where(first_head, acol_a, acol_b)
                xs_p = xs_ref[rows, sl]
                xd = xs_p * dtw
                xd_bf = xd.astype(BF16)
                m_a = (cb * jnp.exp(jnp.where(causal, acol_a - arow_a, -jnp.inf))).astype(BF16)
                m_b = (cb * jnp.exp(jnp.where(causal, acol_b - arow_b, -jnp.inf))).astype(BF16)
                y_diag = jnp.where(first_head, _dot(m_a, xd_bf), _dot(m_b, xd_bf))
                acw_last = acw[CHUNK - 1:CHUNK, :]
                xw = (xd * jnp.exp(acw_last - acw)).astype(BF16)
                st_new = _dot(bg_t, xw)
                prev = state_ref[:, sl]
                y_off = _dot(cg, prev.astype(BF16)) * jnp.exp(acw)
                state_ref[:, sl] = prev * jnp.exp(acw_last) + st_new
                zz = z_ref[rows, sl]
                y = y_diag + y_off + dskip_ref[:, sl] * xs_p
                ys.append(y * (zz * jax.nn.sigmoid(zz)))
            yg = jnp.concatenate(ys, axis=1)
            gw = SSD_INNER // SSD_GROUPS
            yn = _rms(yg, ng_ref[:, g * gw:(g + 1) * gw])
            y_ref[rows, GM_WIDTH + g * gw:GM_WIDTH + (g + 1) * gw] = yn.astype(BF16)
        return carry

    def out_proj(r0, h):
        rows = slice(r0, r0 + gr)
        mixed = _dot(y_ref[rows, :], w_out_ref[...])
        o_ref[rows, :] = h + _rms(mixed, post_g_ref[...])

    starts = [g * gr for g in range(n_split)]
    staged = [in_proj(r0) for r0 in starts]
    for r0, (_, u_pre, v_pre, dt_pre) in zip(starts, staged):
        gmlp(r0, u_pre, v_pre)
        ssd_front(r0, dt_pre)
    ext_ref[0:CONV_HALO, :] = ext_ref[tm:tm + CONV_HALO, :]
    for r0, (h, _, _, _) in zip(starts, staged):
        for c in range(nc):
            chunk_body(r0 // CHUNK + c, 0)
        out_proj(r0, h)


def _hyb_weights(w_in, ln_g, ln_b, w_s, b_s, conv_w, conv_b, dt_bias, a_log, d_skip, norm_g, w_out):
    n = w_in.shape[0]
    pad = (-w_in.shape[2]) % LANES
    w_in_k = jnp.pad(w_in, ((0, 0), (0, 0), (0, pad))).astype(BF16)
    row = lambda a: a.reshape(n, 1, -1)
    pad_h = lambda a: jnp.pad(row(a), ((0, 0), (0, 0), (0, LANES - SSD_HEADS)))
    return (w_in_k, row(ln_g), row(ln_b), w_s, jnp.swapaxes(b_s, 1, 2), conv_w, row(conv_b),
            pad_h(dt_bias), pad_h(a_log), row(jnp.repeat(d_skip, SSD_HEAD_DIM, axis=1)), row(norm_g),
            w_out.astype(BF16))


def _hybrid(h, layer, j, pre_g, weights, post_g):
    b, s, d = h.shape
    tm = min(TM_HYB, s)
    row = pl.BlockSpec((None, tm, d), lambda bi, i: (bi, i, 0))
    return pl.pallas_call(
        functools.partial(_hyb_body, tm=tm, n_split=math.gcd(HYB_SPLIT, tm // CHUNK)),
        grid=(b, s // tm),
        in_specs=([row, _layer_spec(pre_g, layer)] + [_layer_spec(w, j) for w in weights]
                  + [_layer_spec(post_g, layer)]),
        out_specs=row,
        out_shape=jax.ShapeDtypeStruct((b, s, d), F32),
        scratch_shapes=[
            pltpu.VMEM((tm + CONV_HALO, SSD_CONV_CH), F32),
            pltpu.VMEM((SSD_STATE, SSD_INNER), F32),
            pltpu.VMEM((tm, GM_WIDTH + SSD_INNER), BF16),
            pltpu.VMEM((tm, SSD_INNER), F32),
            pltpu.VMEM((tm, SSD_GROUPS * SSD_STATE), F32),
            pltpu.VMEM((tm, SSD_GROUPS * SSD_STATE), F32),
            pltpu.VMEM((tm, SSD_INNER), F32),
            pltpu.VMEM((tm, LANES), F32),
            pltpu.VMEM((tm, LANES), F32),
        ],
        compiler_params=_params(2),
        name="hybrid",
    )(h, pre_g, *weights, post_g)


def kernel(x, p, positions, ffn1_pre_g, ffn1_w_in, ffn1_w_down, ffn1_post_g, mix_pre_g, mix_post_g, ffn2_pre_g, ffn2_w_in, ffn2_w_down, ffn2_post_g, ple_pre_g, ple_w_gate, ple_w_proj, ple_post_g, hyb_w_in, gm_ln_g, gm_ln_b, gm_w_s, gm_b_s, ssd_conv_w, ssd_conv_b, ssd_dt_bias, ssd_a_log, ssd_d, ssd_norm_g, hyb_w_out, mla_w_in, mla_q_norm_g, mla_kv_norm_g, mla_w_uq, mla_w_ukv, mla_w_out):
    b, s, d = x.shape
    t = b * s
    depth = ffn1_w_in.shape[0]
    g3 = lambda a: a.reshape(a.shape[0], 1, -1)
    bf = lambda a: a.astype(BF16)
    ffn1 = (g3(ffn1_pre_g), bf(ffn1_w_in), bf(ffn1_w_down), g3(ffn1_post_g))
    ffn2 = (g3(ffn2_pre_g), bf(ffn2_w_in), bf(ffn2_w_down), g3(ffn2_post_g))
    ple = (p.reshape(depth, t, -1), g3(ple_pre_g), bf(ple_w_gate), bf(ple_w_proj), g3(ple_post_g))
    mix_pre, mix_post = g3(mix_pre_g), g3(mix_post_g)
    hyb_w = _hyb_weights(hyb_w_in, gm_ln_g, gm_ln_b, gm_w_s, gm_b_s, ssd_conv_w, ssd_conv_b,
                         ssd_dt_bias, ssd_a_log, ssd_d, ssd_norm_g, hyb_w_out)
    mla_w = _mla_weights(mla_w_in, mla_w_uq, mla_w_ukv)
    mla_w_o = bf(mla_w_out)
    qc = min(QC_ATTN, s)
    h = x.reshape(t, d)
    for i in range(depth):
        j = i // 2
        h = _ffn(h, i, *ffn1)
        if i % 2 == 0:
            h = _hybrid(h.reshape(b, s, d), i, j, mix_pre, hyb_w, mix_post).reshape(t, d)
            h = _ffn(h, i, *ffn2, ple=ple)
        else:
            q_t, k, v_t = _mla_proj(h, positions, i, j, mix_pre, mla_w, g3(mla_q_norm_g),
                                    g3(mla_kv_norm_g), qc)
            o = _attn(q_t, k, v_t, b, s, qc)
            h = _ffn(h, i, *ffn2, proj=(o.reshape(t, -1), (j, mla_w_o), (i, mix_post)), ple=ple)
    return h.reshape(b, s, d)
```

```python
import functools
import math

import jax
import jax.numpy as jnp
from jax import lax
from jax.experimental import pallas as pl
from jax.experimental.pallas import tpu as pltpu

F32 = jnp.float32
BF16 = jnp.bfloat16

NORM_EPS = 1e-6
LN_EPS = 1e-5

LANES = 128
CHUNK = 128

GM_HEADS = 8
GM_HEAD_DIM = 128
GM_WIDTH = GM_HEADS * GM_HEAD_DIM

SSD_HEAD_DIM = 64
SSD_HEADS = 16
SSD_INNER = SSD_HEADS * SSD_HEAD_DIM
SSD_GROUPS = 2
SSD_STATE = 128
SSD_CONV = 4
SSD_CONV_CH = SSD_INNER + 2 * SSD_GROUPS * SSD_STATE
SSD_PAIRS = SSD_HEADS // 2
CONV_HALO = 8

MLA_HEADS = 16
MLA_NOPE = 128
MLA_ROPE = 64
MLA_V = 128
MLA_Q_LORA = 256
MLA_KV_LORA = 128
MLA_QK = MLA_NOPE + MLA_ROPE
MLA_QK_PAD = 256
ROPE_BASE = 10000.0
OFFSET_ROW = MLA_QK
FAST_LIMIT = 60.0

FFN_CHUNK = 256
FFN_GROUP = 256
HYB_SPLIT = 2
VMEM_LIMIT = 56 * 1024 * 1024

TM_FFN = 512
TM_HYB = 512
QC_ATTN = 512
TQ_ATTN = 2048
TK_ATTN = 1024
KB_UNROLL = 2


def _rms(x, g):
    return x * lax.rsqrt(jnp.mean(x * x, axis=-1, keepdims=True) + NORM_EPS) * g


def _dot(a, b):
    return jnp.dot(a, b, preferred_element_type=F32)


def _const_spec(shape):
    nd = len(shape)
    return pl.BlockSpec(shape, lambda *_: (0,) * nd, pipeline_mode=pl.Buffered(1))


def _layer_spec(stacked, layer):
    zeros = (0,) * (stacked.ndim - 1)
    return pl.BlockSpec((None,) + stacked.shape[1:], lambda *_: (layer,) + zeros,
                        pipeline_mode=pl.Buffered(1))


def _params(n_axes):
    return pltpu.CompilerParams(
        dimension_semantics=("arbitrary",) * n_axes, vmem_limit_bytes=VMEM_LIMIT)


def _ffn_body(*refs, d_ff, with_proj, with_ple):
    refs = list(refs)
    h_ref = refs.pop(0)
    tm = h_ref.shape[0]
    grp = min(FFN_GROUP, tm)
    parts = [slice(r0, r0 + grp) for r0 in range(0, tm, grp)]
    h = [h_ref[p, :] for p in parts]
    if with_proj:
        o_in_ref, w_o_ref, mix_g_ref = refs[:3]
        del refs[:3]
        mixed = [_dot(o_in_ref[p, :], w_o_ref[...]) for p in parts]
        h = [hh + _rms(mm, mix_g_ref[...]) for hh, mm in zip(h, mixed)]
    pre_g_ref, w_in_ref, w_down_ref, post_g_ref = refs[:4]
    del refs[:4]
    if with_ple:
        p_ref, ple_pre_g_ref, wg_ref, wp_ref, ple_post_g_ref = refs[:5]
        del refs[:5]
    o_ref, act_ref = refs

    xn = [_rms(hh, pre_g_ref[...]).astype(BF16) for hh in h]
    for c in range(d_ff // FFN_CHUNK):
        lo = c * FFN_CHUNK
        for p, x in zip(parts, xn):
            g = _dot(x, w_in_ref[:, lo:lo + FFN_CHUNK])
            u = _dot(x, w_in_ref[:, d_ff + lo:d_ff + lo + FFN_CHUNK])
            act_ref[p, lo:lo + FFN_CHUNK] = (g * jax.nn.sigmoid(g) * u).astype(BF16)
    f = [_dot(act_ref[p, :], w_down_ref[...]) for p in parts]
    if with_ple:
        proj = [_dot(p_ref[p, :].astype(BF16), wp_ref[...]) for p in parts]
    h = [hh + 0.5 * _rms(ff, post_g_ref[...]) for hh, ff in zip(h, f)]
    if with_ple:
        xn = [_rms(hh, ple_pre_g_ref[...]).astype(BF16) for hh in h]
        gate = [jax.nn.sigmoid(_dot(x, wg_ref[...])) for x in xn]
        h = [hh + _rms(gg * pp, ple_post_g_ref[...]) for hh, gg, pp in zip(h, gate, proj)]
    for p, hh in zip(parts, h):
        o_ref[p, :] = hh


def _ffn(h, layer, pre_g, w_in, w_down, post_g, proj=None, ple=None):
    t, d = h.shape
    d_ff = w_down.shape[1]
    tm = min(TM_FFN if proj is not None else 2 * TM_FFN, t)
    row = pl.BlockSpec((tm, d), lambda i: (i, 0))
    args, specs = [h], [row]
    if proj is not None:
        o_in, (lw, w_o), (lg, mix_g) = proj
        args += [o_in, w_o, mix_g]
        specs += [pl.BlockSpec((tm, o_in.shape[1]), lambda i: (i, 0)), _layer_spec(w_o, lw),
                  _layer_spec(mix_g, lg)]
    stacked = [pre_g, w_in, w_down, post_g]
    args += stacked
    specs += [_layer_spec(a, layer) for a in stacked]
    if ple is not None:
        args += list(ple)
        specs += [pl.BlockSpec((None, tm, ple[0].shape[2]), lambda i: (layer, i, 0))]
        specs += [_layer_spec(a, layer) for a in ple[1:]]
    return pl.pallas_call(
        functools.partial(_ffn_body, d_ff=d_ff, with_proj=proj is not None, with_ple=ple is not None),
        grid=(t // tm,),
        in_specs=specs,
        out_specs=row,
        out_shape=jax.ShapeDtypeStruct((t, d), F32),
        scratch_shapes=[pltpu.VMEM((tm, d_ff), BF16)],
        compiler_params=_params(1),
        name="ffn",
    )(*args)


def _mla_proj_body(h_ref, pos_ref, inv_ref, pre_g_ref, w_in_ref, qg_ref, kvg_ref,
                   wq_t_ref, wk_ref, wv_t_ref, q_t_ref, k_ref, v_t_ref, *, q_scale):
    tm = h_ref.shape[0]
    ang = inv_ref[...] * pos_ref[...].astype(F32)
    cos32 = jnp.cos(ang)
    sin32 = jnp.sin(ang)
    pad = jnp.zeros((LANES - MLA_ROPE, tm), F32)
    cos_t = jnp.concatenate([cos32, cos32, pad], axis=0)
    sin_t = jnp.concatenate([-sin32, sin32, pad], axis=0)
    cos_n = cos_t.T
    sin_n = sin_t.T
    hn = _rms(h_ref[...], pre_g_ref[...]).astype(BF16)
    c = _dot(hn, w_in_ref[...])
    cq = _rms(c[:, :MLA_Q_LORA], qg_ref[...])
    ckv = _rms(c[:, MLA_Q_LORA:MLA_Q_LORA + MLA_KV_LORA], kvg_ref[...])
    o = MLA_Q_LORA + MLA_KV_LORA
    one = jnp.where(lax.broadcasted_iota(jnp.int32, (1, LANES), 1) == OFFSET_ROW - MLA_NOPE, 1.0, 0.0)
    kr = (c[:, o:o + LANES] * cos_n + c[:, o + LANES:o + 2 * LANES] * sin_n + one).astype(BF16)
    cq_t = cq.T.astype(BF16)
    ckv_t = ckv.T.astype(BF16)
    kn = _dot(ckv.astype(BF16), wk_ref[...])
    v_t = _dot(wv_t_ref[...], ckv_t)
    for hd in range(MLA_HEADS):
        k_ref[:, hd * MLA_QK_PAD:hd * MLA_QK_PAD + MLA_NOPE] = (
            kn[:, hd * MLA_NOPE:(hd + 1) * MLA_NOPE].astype(BF16))
        k_ref[:, hd * MLA_QK_PAD + MLA_NOPE:(hd + 1) * MLA_QK_PAD] = kr
        v_t_ref[hd] = v_t[hd * MLA_V:(hd + 1) * MLA_V, :].astype(BF16)
        qh = _dot(wq_t_ref[hd], cq_t)
        qr = qh[MLA_NOPE:MLA_NOPE + LANES] * cos_t + qh[MLA_NOPE + LANES:] * sin_t
        q_t_ref[hd, 0:MLA_NOPE, :] = (qh[:MLA_NOPE] * q_scale).astype(BF16)
        q_t_ref[hd, MLA_NOPE:MLA_QK_PAD, :] = (qr * q_scale).astype(BF16)


def _mla_weights(w_in, w_uq, w_ukv):
    n = w_in.shape[0]
    half = MLA_ROPE // 2

    def rope_slabs(w):
        x1, x2 = w[..., :half], w[..., half:]
        z = jnp.zeros(w.shape[:-1] + (LANES - MLA_ROPE,), w.dtype)
        return jnp.concatenate([x1, x2, z, x2, x1, z], axis=-1)

    o = MLA_Q_LORA + MLA_KV_LORA
    w_in_k = jnp.concatenate([w_in[..., :o], rope_slabs(w_in[..., o:])], axis=-1).astype(BF16)
    wq = w_uq.reshape(n, MLA_Q_LORA, MLA_HEADS, MLA_QK)
    wq = jnp.concatenate([wq[..., :MLA_NOPE], rope_slabs(wq[..., MLA_NOPE:])], axis=-1)
    wq_t = jnp.transpose(wq, (0, 2, 3, 1)).astype(BF16)
    wkv = w_ukv.reshape(n, MLA_KV_LORA, MLA_HEADS, MLA_NOPE + MLA_V)
    wk = wkv[..., :MLA_NOPE].reshape(n, MLA_KV_LORA, MLA_HEADS * MLA_NOPE).astype(BF16)
    wv = wkv[..., MLA_NOPE:].reshape(n, MLA_KV_LORA, MLA_HEADS * MLA_V)
    wv_t = jnp.swapaxes(wv, 1, 2).astype(BF16)
    return w_in_k, wq_t, wk, wv_t


def _mla_proj(h, pos, layer, j, pre_g, weights, q_norm_g, kv_norm_g, qc):
    t, d = h.shape
    half = MLA_ROPE // 2
    inv = 1.0 / (ROPE_BASE ** (jnp.arange(0, MLA_ROPE, 2, dtype=F32) / MLA_ROPE))
    q_scale = (MLA_QK ** -0.5) * math.log2(math.e)
    w_in_k, wq_t, wk, wv_t = weights
    n = t // qc
    return pl.pallas_call(
        functools.partial(_mla_proj_body, q_scale=q_scale),
        grid=(n,),
        in_specs=[pl.BlockSpec((qc, d), lambda i: (i, 0)), pl.BlockSpec((1, qc), lambda i: (0, i)),
                  _const_spec((half, 1)), _layer_spec(pre_g, layer), _layer_spec(w_in_k, j),
                  _layer_spec(q_norm_g, j), _layer_spec(kv_norm_g, j),
                  _layer_spec(wq_t, j), _layer_spec(wk, j), _layer_spec(wv_t, j)],
        out_specs=[pl.BlockSpec((None, MLA_HEADS, MLA_QK_PAD, qc), lambda i: (i, 0, 0, 0)),
                   pl.BlockSpec((qc, MLA_HEADS * MLA_QK_PAD), lambda i: (i, 0)),
                   pl.BlockSpec((None, MLA_HEADS, MLA_V, qc), lambda i: (i, 0, 0, 0))],
        out_shape=[jax.ShapeDtypeStruct((n, MLA_HEADS, MLA_QK_PAD, qc), BF16),
                   jax.ShapeDtypeStruct((t, MLA_HEADS * MLA_QK_PAD), BF16),
                   jax.ShapeDtypeStruct((n, MLA_HEADS, MLA_V, qc), BF16)],
        compiler_params=_params(1),
        name="mla_proj",
    )(h, pos.reshape(1, t), inv.reshape(half, 1), pre_g, w_in_k, q_norm_g, kv_norm_g, wq_t, wk, wv_t)


def _qk_t(k, q_t, mask):
    s_t = _dot(k, q_t)
    if mask is not None:
        s_t = jnp.where(mask, s_t, -jnp.inf)
    return s_t


def _softmax_pv_t(s_t, v_t, m_ref, l_ref, acc_ref, r):
    m_prev = m_ref[r]
    m_new = jnp.maximum(m_prev, jnp.max(s_t, axis=0, keepdims=True))
    alpha = jnp.exp2(m_prev - m_new)
    p_t = jnp.exp2(s_t - m_new)
    l_ref[r] = alpha * l_ref[r] + jnp.sum(p_t, axis=0, keepdims=True)
    acc_ref[r] = _dot(v_t, p_t.astype(BF16)) + alpha * acc_ref[r]
    m_ref[r] = m_new


def _lagged_pv_t(s_t, v_t, m_ref, l_ref, acc_ref, qa_ref, excess_ref, r):
    p_t = jnp.exp2(s_t)
    top = jnp.max(s_t, axis=0, keepdims=True)
    l_new = l_ref[r] + jnp.sum(p_t, axis=0, keepdims=True)
    acc_new = _dot(v_t, p_t.astype(BF16)) + acc_ref[r]
    m_old = m_ref[r]
    m_new = (m_old + jnp.maximum(top, 0.0)).astype(BF16).astype(F32)
    alpha = jnp.exp2(m_old - m_new)
    l_ref[r] = l_new * alpha
    acc_ref[r] = acc_new * alpha
    m_ref[r] = m_new
    qa_ref[r, OFFSET_ROW:OFFSET_ROW + 1, :] = (-m_new).astype(BF16)
    excess_ref[...] = jnp.maximum(excess_ref[...], top)


def _attn_body(q_t_ref, k_ref, v_t_ref, o_ref, m_ref, l_ref, acc_ref, qa_ref, excess_ref,
               *, tq, tk, qc, unroll):
    i = pl.program_id(2)
    n_sub = tq // qc
    kc = tk // qc
    n_blocks = i * (tq // tk) // unroll
    d0 = pl.multiple_of(i * tq, tq)
    krow = lax.broadcasted_iota(jnp.int32, (qc, qc), 0)
    qcol = lax.broadcasted_iota(jnp.int32, (qc, qc), 1)
    tri = krow <= qcol

    def reset():
        m_ref[...] = jnp.full(m_ref.shape, -jnp.inf, F32)
        l_ref[...] = jnp.zeros(l_ref.shape, F32)
        acc_ref[...] = jnp.zeros(acc_ref.shape, F32)

    def exact(s_t, v_t, r):
        _softmax_pv_t(s_t, v_t, m_ref, l_ref, acc_ref, r)

    def lagged(s_t, v_t, r):
        _lagged_pv_t(s_t, v_t, m_ref, l_ref, acc_ref, qa_ref, excess_ref, r)

    def pipelined(items, update):
        s_t = items[0][0]()
        for n, (_, v_t, r) in enumerate(items):
            more = n + 1 < len(items)
            ahead = more and items[n + 1][2] != r
            s_next = items[n + 1][0]() if ahead else None
            update(s_t, v_t, r)
            s_t = items[n + 1][0]() if more and not ahead else s_next

    def tile_keys(c0, c1):
        k = k_ref[pl.ds(pl.multiple_of(d0 + c0 * qc, qc), (c1 - c0) * qc), :]
        v_t = jnp.concatenate([v_t_ref[i * n_sub + c] for c in range(c0, c1)], axis=1)
        return k, v_t

    def block_keys(jb):
        k = k_ref[pl.ds(pl.multiple_of(jb * tk, tk), tk), :]
        v_t = jnp.concatenate([v_t_ref[jb * kc + c] for c in range(kc)], axis=1)
        return k, v_t

    def block_items(j, q_ref):
        items = []
        for u in range(unroll):
            k, v_t = block_keys(j * unroll + u)
            items += [(lambda k=k, r=r: _qk_t(k, q_ref[r], None), v_t, r) for r in range(n_sub)]
        return items

    def finish():
        for r in range(n_sub):
            o_t = acc_ref[r] / l_ref[r]
            o_ref[r * qc:(r + 1) * qc, :] = o_t.T.astype(o_ref.dtype)

    reset()
    excess_ref[...] = jnp.zeros(excess_ref.shape, F32)

    def own_scores(k, r):
        s_t = _qk_t(k, q_t_ref[r], None)
        m0 = s_t[0:1, :].astype(BF16).astype(F32)
        m_ref[r] = m0
        qa_ref[r] = q_t_ref[r]
        return jnp.where(tri, s_t - m0, -jnp.inf)

    diag = []
    for r in reversed(range(n_sub)):
        k, v_t = tile_keys(r, r + 1)
        diag.append((lambda k=k, r=r: own_scores(k, r), v_t, r))
    for r in reversed(range(1, n_sub)):
        for c0 in range(0, r, kc):
            k, v_t = tile_keys(c0, min(c0 + kc, r))
            diag.append((lambda k=k, r=r: _qk_t(k, qa_ref[r], None), v_t, r))
    pipelined(diag, lagged)


    def lagged_body(j, carry):
        pipelined(block_items(j, qa_ref), lagged)
        return carry

    lax.fori_loop(0, n_blocks, lagged_body, 0)
    finish()

    @pl.when(jnp.max(excess_ref[...]) > FAST_LIMIT)
    def _():
        reset()

        def exact_body(j, carry):
            pipelined(block_items(j, q_t_ref), exact)
            return carry

        lax.fori_loop(0, n_blocks, exact_body, 0)
        diag = []
        for r in range(n_sub):
            k, v_t = tile_keys(0, r + 1)
            mask = tri if r == 0 else jnp.concatenate([jnp.ones((r * qc, qc), jnp.bool_), tri], axis=0)
            diag.append((lambda k=k, r=r, mask=mask: _qk_t(k, q_t_ref[r], mask), v_t, r))
        pipelined(diag, exact)
        finish()


def _attn(q_t, k, v_t, b, s, qc):
    tq = min(TQ_ATTN, s)
    tk = min(TK_ATTN, tq)
    n_sub = tq // qc
    unroll = math.gcd(KB_UNROLL, tq // tk)
    return pl.pallas_call(
        functools.partial(_attn_body, tq=tq, tk=tk, qc=qc, unroll=unroll),
        grid=(b, MLA_HEADS, s // tq),
        in_specs=[pl.BlockSpec((n_sub, None, MLA_QK_PAD, qc),
                               lambda bi, h, i: (bi * (s // tq) + i, h, 0, 0)),
                  pl.BlockSpec((None, s, MLA_QK_PAD), lambda bi, h, i: (bi, 0, h)),
                  pl.BlockSpec((s // qc, None, MLA_V, qc), lambda bi, h, i: (bi, h, 0, 0))],
        out_specs=pl.BlockSpec((None, tq, MLA_V), lambda bi, h, i: (bi, i, h)),
        out_shape=jax.ShapeDtypeStruct((b, s, MLA_HEADS * MLA_V), BF16),
        scratch_shapes=[pltpu.VMEM((n_sub, 1, qc), F32), pltpu.VMEM((n_sub, 1, qc), F32),
                        pltpu.VMEM((n_sub, MLA_V, qc), F32),
                        pltpu.VMEM((n_sub, MLA_QK_PAD, qc), BF16),
                        pltpu.VMEM((1, qc), F32)],
        compiler_params=_params(3),
        name="attn",
    )(q_t, k.reshape(b, s, MLA_HEADS * MLA_QK_PAD), v_t)


def _gelu_tanh(x):
    c = 2.0 * math.sqrt(2.0 / math.pi)
    return x * jax.nn.sigmoid(x * (c + (c * 0.044715) * (x * x)))


def _softplus(x):
    return jnp.maximum(x, 0.0) + jnp.log1p(jnp.exp(-jnp.abs(x)))


def _hyb_body(h_ref, pre_g_ref, w_in_ref, ln_g_ref, ln_b_ref, ws_ref, bs_ref,
              conv_w_ref, conv_b_ref, dt_bias_ref, a_log_ref, dskip_ref, ng_ref,
              w_out_ref, post_g_ref, o_ref,
              ext_ref, state_ref, y_ref, xs_ref, b_ref, c_ref, z_ref, dt_ref, da_ref, *, tm, n_split):
    gr = tm // n_split
    nc = gr // CHUNK
    t_idx = pl.program_id(1)

    @pl.when(t_idx == 0)
    def _():
        ext_ref[0:CONV_HALO, :] = jnp.zeros((CONV_HALO, SSD_CONV_CH), F32)
        state_ref[...] = jnp.zeros(state_ref.shape, F32)

    row = lax.broadcasted_iota(jnp.int32, (CHUNK, CHUNK), 0)
    col = lax.broadcasted_iota(jnp.int32, (CHUNK, CHUNK), 1)
    causal = col <= row

    def in_proj(r0):
        rows = slice(r0, r0 + gr)
        h = h_ref[rows, :]
        hn = _rms(h, pre_g_ref[...]).astype(BF16)
        u_pre = _dot(hn, w_in_ref[:, 0:GM_WIDTH])
        v_pre = _dot(hn, w_in_ref[:, GM_WIDTH:2 * GM_WIDTH])
        o = 2 * GM_WIDTH
        z_ref[rows, :] = _dot(hn, w_in_ref[:, o:o + SSD_INNER])
        o += SSD_INNER
        ext_ref[CONV_HALO + r0:CONV_HALO + r0 + gr, :] = _dot(hn, w_in_ref[:, o:o + SSD_CONV_CH])
        o += SSD_CONV_CH
        dt_pre = _dot(hn, w_in_ref[:, o:o + LANES])
        return h, u_pre, v_pre, dt_pre

    def gmlp(r0, u_pre, v_pre):
        u = _gelu_tanh(u_pre)
        v = _gelu_tanh(v_pre)
        for hd in range(GM_HEADS):
            sl = slice(hd * GM_HEAD_DIM, (hd + 1) * GM_HEAD_DIM)
            vh = v[:, sl]
            mu = jnp.mean(vh, axis=-1, keepdims=True)
            xc = vh - mu
            var = jnp.mean(xc * xc, axis=-1, keepdims=True)
            vn = (xc * lax.rsqrt(var + LN_EPS) * ln_g_ref[:, sl] + ln_b_ref[:, sl]).astype(BF16)
            rhs = jnp.concatenate([vn[c * CHUNK:(c + 1) * CHUNK] for c in range(nc)], axis=1)
            w = jnp.where(causal, ws_ref[hd], 0.0).astype(BF16)
            mixed = _dot(w, rhs)
            bias = bs_ref[:, hd:hd + 1]
            for c in range(nc):
                y_ref[r0 + c * CHUNK:r0 + (c + 1) * CHUNK, sl] = (
                    u[c * CHUNK:(c + 1) * CHUNK, sl] * (mixed[:, c * CHUNK:(c + 1) * CHUNK] + bias)
                ).astype(BF16)

    def ssd_front(r0, dt_pre):
        rows = slice(r0, r0 + gr)
        dt = _softplus(dt_pre + dt_bias_ref[...])
        dt_ref[rows, :] = dt
        da_ref[rows, :] = dt * (-jnp.exp(a_log_ref[...]))
        win = ext_ref[r0:r0 + CONV_HALO + gr, :]
        part = conv_w_ref[0:1, :] * win
        for kk in range(1, SSD_CONV):
            part = pltpu.roll(part, 1, axis=0) + conv_w_ref[kk:kk + 1, :] * win
        conv = part[CONV_HALO:, :] + conv_b_ref[...]
        xbc = conv * jax.nn.sigmoid(conv)
        xs_ref[rows, :] = xbc[:, :SSD_INNER]
        b_ref[rows, :] = xbc[:, SSD_INNER:SSD_INNER + SSD_GROUPS * SSD_STATE]
        c_ref[rows, :] = xbc[:, SSD_INNER + SSD_GROUPS * SSD_STATE:]

    tril = jnp.where(causal, 1.0, 0.0).astype(F32)
    lane = lax.broadcasted_iota(jnp.int32, (CHUNK, LANES), 1)
    first_head = lane < SSD_HEAD_DIM

    def chunk_body(c, carry):
        rows = pl.ds(c * CHUNK, CHUNK)
        dt_c = dt_ref[rows, :]
        acs = jnp.dot(tril, da_ref[rows, :], precision=lax.Precision.HIGHEST,
                      preferred_element_type=F32)
        acs_t = acs.T
        bc = b_ref[rows, :]
        cc = c_ref[rows, :]
        for g in range(SSD_GROUPS):
            gs = slice(g * SSD_STATE, (g + 1) * SSD_STATE)
            bg_t = bc[:, gs].T.astype(BF16)
            cg = cc[:, gs].astype(BF16)
            cb = _dot(cg, bg_t)
            ys = []
            for pj in range(SSD_PAIRS // SSD_GROUPS):
                j = g * (SSD_PAIRS // SSD_GROUPS) + pj
                ha, hb = 2 * j, 2 * j + 1
                sl = slice(j * LANES, (j + 1) * LANES)
                acol_a = jnp.broadcast_to(acs[:, ha:ha + 1], (CHUNK, LANES))
                acol_b = jnp.broadcast_to(acs[:, hb:hb + 1], (CHUNK, LANES))
                arow_a = jnp.broadcast_to(acs_t[ha:ha + 1, :], (CHUNK, CHUNK))
                arow_b = jnp.broadcast_to(acs_t[hb:hb + 1, :], (CHUNK, CHUNK))
                dtw = jnp.where(first_head, jnp.broadcast_to(dt_c[:, ha:ha + 1], (CHUNK, LANES)),
                                jnp.broadcast_to(dt_c[:, hb:hb + 1], (CHUNK, LANES)))
                acw = jnp.where(first_head, acol_a, acol_b)
                xs_p = xs_ref[rows, sl]
                xd = xs_p * dtw
                xd_bf = xd.astype(BF16)
                m_a = (cb * jnp.exp(jnp.where(causal, acol_a - arow_a, -jnp.inf))).astype(BF16)
                m_b = (cb * jnp.exp(jnp.where(causal, acol_b - arow_b, -jnp.inf))).astype(BF16)
                y_diag = jnp.where(first_head, _dot(m_a, xd_bf), _dot(m_b, xd_bf))
                acw_last = acw[CHUNK - 1:CHUNK, :]
                xw = (xd * jnp.exp(acw_last - acw)).astype(BF16)
                st_new = _dot(bg_t, xw)
                prev = state_ref[:, sl]
                y_off = _dot(cg, prev.astype(BF16)) * jnp.exp(acw)
                state_ref[:, sl] = prev * jnp.exp(acw_last) + st_new
                zz = z_ref[rows, sl]
                y = y_diag + y_off + dskip_ref[:, sl] * xs_p
                ys.append(y * (zz * jax.nn.sigmoid(zz)))
            yg = jnp.concatenate(ys, axis=1)
            gw = SSD_INNER // SSD_GROUPS
            yn = _rms(yg, ng_ref[:, g * gw:(g + 1) * gw])
            y_ref[rows, GM_WIDTH + g * gw:GM_WIDTH + (g + 1) * gw] = yn.astype(BF16)
        return carry

    def out_proj(r0, h):
        rows = slice(r0, r0 + gr)
        mixed = _dot(y_ref[rows, :], w_out_ref[...])
        o_ref[rows, :] = h + _rms(mixed, post_g_ref[...])

    starts = [g * gr for g in range(n_split)]
    staged = [in_proj(r0) for r0 in starts]
    for r0, (_, u_pre, v_pre, dt_pre) in zip(starts, staged):
        gmlp(r0, u_pre, v_pre)
        ssd_front(r0, dt_pre)
    ext_ref[0:CONV_HALO, :] = ext_ref[tm:tm + CONV_HALO, :]
    for r0, (h, _, _, _) in zip(starts, staged):
        for c in range(nc):
            chunk_body(r0 // CHUNK + c, 0)
        out_proj(r0, h)


def _hyb_weights(w_in, ln_g, ln_b, w_s, b_s, conv_w, conv_b, dt_bias, a_log, d_skip, norm_g, w_out):
    n = w_in.shape[0]
    pad = (-w_in.shape[2]) % LANES
    w_in_k = jnp.pad(w_in, ((0, 0), (0, 0), (0, pad))).astype(BF16)
    row = lambda a: a.reshape(n, 1, -1)
    pad_h = lambda a: jnp.pad(row(a), ((0, 0), (0, 0), (0, LANES - SSD_HEADS)))
    return (w_in_k, row(ln_g), row(ln_b), w_s, jnp.swapaxes(b_s, 1, 2), conv_w, row(conv_b),
            pad_h(dt_bias), pad_h(a_log), row(jnp.repeat(d_skip, SSD_HEAD_DIM, axis=1)), row(norm_g),
            w_out.astype(BF16))


def _hybrid(h, layer, j, pre_g, weights, post_g):
    b, s, d = h.shape
    tm = min(TM_HYB, s)
    row = pl.BlockSpec((None, tm, d), lambda bi, i: (bi, i, 0))
    return pl.pallas_call(
        functools.partial(_hyb_body, tm=tm, n_split=math.gcd(HYB_SPLIT, tm // CHUNK)),
        grid=(b, s // tm),
        in_specs=([row, _layer_spec(pre_g, layer)] + [_layer_spec(w, j) for w in weights]
                  + [_layer_spec(post_g, layer)]),
        out_specs=row,
        out_shape=jax.ShapeDtypeStruct((b, s, d), F32),
        scratch_shapes=[
            pltpu.VMEM((tm + CONV_HALO, SSD_CONV_CH), F32),
            pltpu.VMEM((SSD_STATE, SSD_INNER), F32),
            pltpu.VMEM((tm, GM_WIDTH + SSD_INNER), BF16),
            pltpu.VMEM((tm, SSD_INNER), F32),
            pltpu.VMEM((tm, SSD_GROUPS * SSD_STATE), F32),
            pltpu.VMEM((tm, SSD_GROUPS * SSD_STATE), F32),
            pltpu.VMEM((tm, SSD_INNER), F32),
            pltpu.VMEM((tm, LANES), F32),
            pltpu.VMEM((tm, LANES), F32),
        ],
        compiler_params=_params(2),
        name="hybrid",
    )(h, pre_g, *weights, post_g)


def kernel(x, p, positions, ffn1_pre_g, ffn1_w_in, ffn1_w_down, ffn1_post_g, mix_pre_g, mix_post_g, ffn2_pre_g, ffn2_w_in, ffn2_w_down, ffn2_post_g, ple_pre_g, ple_w_gate, ple_w_proj, ple_post_g, hyb_w_in, gm_ln_g, gm_ln_b, gm_w_s, gm_b_s, ssd_conv_w, ssd_conv_b, ssd_dt_bias, ssd_a_log, ssd_d, ssd_norm_g, hyb_w_out, mla_w_in, mla_q_norm_g, mla_kv_norm_g, mla_w_uq, mla_w_ukv, mla_w_out):
    b, s, d = x.shape
    t = b * s
    depth = ffn1_w_in.shape[0]
    g3 = lambda a: a.reshape(a.shape[0], 1, -1)
    bf = lambda a: a.astype(BF16)
    ffn1 = (g3(ffn1_pre_g), bf(ffn1_w_in), bf(ffn1_w_down), g3(ffn1_post_g))
    ffn2 = (g3(ffn2_pre_g), bf(ffn2_w_in), bf(ffn2_w_down), g3(ffn2_post_g))
    ple = (p.reshape(depth, t, -1), g3(ple_pre_g), bf(ple_w_gate), bf(ple_w_proj), g3(ple_post_g))
    mix_pre, mix_post = g3(mix_pre_g), g3(mix_post_g)
    hyb_w = _hyb_weights(hyb_w_in, gm_ln_g, gm_ln_b, gm_w_s, gm_b_s, ssd_conv_w, ssd_conv_b,
                         ssd_dt_bias, ssd_a_log, ssd_d, ssd_norm_g, hyb_w_out)
    mla_w = _mla_weights(mla_w_in, mla_w_uq, mla_w_ukv)
    mla_w_o = bf(mla_w_out)
    qc = min(QC_ATTN, s)
    h = x.reshape(t, d)
    for i in range(depth):
        j = i // 2
        h = _ffn(h, i, *ffn1)
        if i % 2 == 0:
            h = _hybrid(h.reshape(b, s, d), i, j, mix_pre, hyb_w, mix_post).reshape(t, d)
            h = _ffn(h, i, *ffn2, ple=ple)
        else:
            q_t, k, v_t = _mla_proj(h, positions, i, j, mix_pre, mla_w, g3(mla_q_norm_g),
                                    g3(mla_kv_norm_g), qc)
            o = _attn(q_t, k, v_t, b, s, qc)
            h = _ffn(h, i, *ffn2, proj=(o.reshape(t, -1), (j, mla_w_o), (i, mix_post)), ple=ple)
    return h.reshape(b, s, d)
```
